```python
import math
import jax, jax.numpy as jnp
from jax import lax
import numpy as np

D_MODEL = 1024
BATCH = 8
SEQ = 4096
DEPTH = 2

HEAD_DIM = 64
A_WIDTH = D_MODEL // 2
A_HEADS = A_WIDTH // HEAD_DIM
B_WIDTH = D_MODEL - A_WIDTH
DECAY_RANK = 64
ICL_RANK = 64
GATE_RANK = 128
CONV_WIDTH = 31
GN_EPS = 64e-5
A_PROJ = 3 * A_WIDTH + DECAY_RANK + ICL_RANK + GATE_RANK
EVEN_IN = A_PROJ + 2 * B_WIDTH
C_HEADS = D_MODEL // (2 * HEAD_DIM)
ODD_IN = 3 * D_MODEL
Q_BLOCK = 128
ROPE_THETA = 10000.0
ATTN_SCALE = HEAD_DIM ** -0.5
D_FF = 2816
N_EVEN = (DEPTH + 1) // 2
N_ODD = DEPTH // 2

kernel_name = "hybrid_rwkv7_conformer_diffattn_macaron"


def rmsnorm(x, g, eps=1e-6):
    xf = x.astype(jnp.float32)
    y = xf * lax.rsqrt(jnp.mean(xf * xf, axis=-1, keepdims=True) + eps)
    return (y * g.astype(jnp.float32)).astype(x.dtype)


def layernorm(x, g, b, eps=1e-5):
    xf = x.astype(jnp.float32)
    mu = jnp.mean(xf, axis=-1, keepdims=True)
    var = jnp.mean(jnp.square(xf - mu), axis=-1, keepdims=True)
    y = (xf - mu) * lax.rsqrt(var + eps)
    return (y * g.astype(jnp.float32) + b.astype(jnp.float32)).astype(x.dtype)


def swiglu(h, w_gate, w_up, w_down):
    return (jax.nn.silu(h @ w_gate) * (h @ w_up)) @ w_down


def token_shift(z):
    return jnp.pad(z, ((0, 0), (1, 0), (0, 0)))[:, :-1]


def rope_tables(seq):
    inv = 1.0 / (ROPE_THETA ** (jnp.arange(0, HEAD_DIM, 2, dtype=jnp.float32) / HEAD_DIM))
    ang = jnp.arange(seq, dtype=jnp.float32)[:, None] * inv[None, :]
    return jnp.cos(ang), jnp.sin(ang)


def apply_rope(x, cos, sin):
    xf = x.astype(jnp.float32)
    c = cos[None, :, None, None, :]
    s = sin[None, :, None, None, :]
    x1, x2 = xf[..., : HEAD_DIM // 2], xf[..., HEAD_DIM // 2:]
    return jnp.concatenate([x1 * c - x2 * s, x2 * c + x1 * s], axis=-1)


def wkv7_scan(r, w, k, v, a, b):
    Bsz, _, H, N = r.shape
    xs = tuple(jnp.moveaxis(t, 1, 0) for t in (r, w, k, v, a, b))

    def step(state, inp):
        r_t, w_t, k_t, v_t, a_t, b_t = inp
        sa = jnp.einsum('bhvk,bhk->bhv', state, a_t)
        state = (state * w_t[:, :, None, :]
                 + sa[..., None] * b_t[:, :, None, :]
                 + v_t[..., None] * k_t[:, :, None, :])
        y = jnp.einsum('bhvk,bhk->bhv', state, r_t)
        return state, y

    s0 = jnp.zeros((Bsz, H, N, N), jnp.float32)
    _, ys = lax.scan(step, s0, xs)
    return jnp.moveaxis(ys, 0, 1)


def rwkv_conv_mixer(h, w_in, mu, w0, w2, a0, a2, g2, k_k, k_a, r_k, ln_w, ln_b,
                    glu_bias, dw, dw_bias, cln_w, cln_b, w_out):
    Bsz, S, _ = h.shape
    f32 = jnp.float32
    z = h @ w_in
    za = z[..., :A_PROJ]
    za = za + (token_shift(za) - za) * mu
    cuts = [A_WIDTH, 2 * A_WIDTH, 3 * A_WIDTH, 3 * A_WIDTH + DECAY_RANK,
            3 * A_WIDTH + DECAY_RANK + ICL_RANK]
    r, k, v, wd, ad, gd = jnp.split(za, cuts, axis=-1)
    w_log = -jax.nn.softplus(-(w0 + jnp.tanh(wd) @ w2)) - 0.5
    decay = jnp.exp(-jnp.exp(w_log.astype(f32)))
    a = jax.nn.sigmoid(a0 + ad @ a2)
    g = jax.nn.sigmoid(gd) @ g2

    def heads(t):
        return t.reshape(Bsz, S, A_HEADS, HEAD_DIM).astype(f32)

    kk = heads(k * k_k)
    kk = kk / jnp.maximum(jnp.sqrt(jnp.sum(kk * kk, axis=-1, keepdims=True)), 1e-12)
    k = k * (1.0 + (a - 1.0) * k_a)
    rh, kh, vh, ah, wh = heads(r), heads(k), heads(v), heads(a), heads(decay)
    y = wkv7_scan(rh, wh, kh, vh, -kk, kk * ah)
    mean = jnp.mean(y, axis=-1, keepdims=True)
    var = jnp.mean(jnp.square(y - mean), axis=-1, keepdims=True)
    y = ((y - mean) * lax.rsqrt(var + GN_EPS)).reshape(Bsz, S, A_WIDTH)
    y = y * ln_w.astype(f32) + ln_b.astype(f32)
    bonus = (jnp.sum(rh * kh * r_k.astype(f32), axis=-1, keepdims=True) * vh).reshape(Bsz, S, A_WIDTH)
    y_a = ((y + bonus) * g.astype(f32)).astype(h.dtype)
    u = z[..., A_PROJ:] + glu_bias
    gl = u[..., :B_WIDTH] * jax.nn.sigmoid(u[..., B_WIDTH:])
    c = lax.conv_general_dilated(gl, dw[:, None, :], window_strides=(1,),
                                 padding=[(CONV_WIDTH - 1, 0)],
                                 dimension_numbers=('NWC', 'WIO', 'NWC'),
                                 feature_group_count=B_WIDTH) + dw_bias
    y_b = jax.nn.silu(layernorm(c, cln_w, cln_b))
    return jnp.concatenate([y_a, y_b], axis=-1) @ w_out


def diff_attention(h, w_in, q_norm, k_norm, lq1, lk1, lq2, lk2, sub_norm, w_out,
                   lam_init, cos, sin):
    Bsz, S, _ = h.shape
    f32 = jnp.float32
    z = h @ w_in
    q = z[..., :D_MODEL].reshape(Bsz, S, C_HEADS, 2, HEAD_DIM)
    k = z[..., D_MODEL:2 * D_MODEL].reshape(Bsz, S, C_HEADS, 2, HEAD_DIM)
    v = z[..., 2 * D_MODEL:].reshape(Bsz, S, C_HEADS, 2 * HEAD_DIM)
    q = apply_rope(rmsnorm(q, q_norm), cos, sin).transpose(3, 0, 2, 1, 4)
    k = apply_rope(rmsnorm(k, k_norm), cos, sin).transpose(3, 0, 2, 1, 4)
    vf = v.transpose(0, 2, 1, 3).astype(f32)
    lam = (jnp.exp(jnp.sum(lq1.astype(f32) * lk1.astype(f32)))
           - jnp.exp(jnp.sum(lq2.astype(f32) * lk2.astype(f32))) + lam_init)
    nb = S // Q_BLOCK
    qb = q.reshape(2, Bsz, C_HEADS, nb, Q_BLOCK, HEAD_DIM).transpose(3, 0, 1, 2, 4, 5)
    kpos = jnp.arange(S)

    def block(args):
        qi, i = args
        s = jnp.einsum('cbhqd,cbhkd->cbhqk', qi, k) * ATTN_SCALE
        qpos = i * Q_BLOCK + jnp.arange(Q_BLOCK)
        mask = kpos[None, :] <= qpos[:, None]
        p = jax.nn.softmax(jnp.where(mask, s, -jnp.inf), axis=-1)
        attn = p[0] - lam * p[1]
        return jnp.einsum('bhqk,bhkv->bhqv', attn, vf)

    o = lax.map(block, (qb, jnp.arange(nb)))
    o = o.transpose(1, 0, 3, 2, 4).reshape(Bsz, S, C_HEADS, 2 * HEAD_DIM)
    o = rmsnorm(o, sub_norm, eps=1e-5) * (1.0 - lam_init)
    return o.reshape(Bsz, S, D_MODEL).astype(h.dtype) @ w_out


def setup_inputs(seed: int = 0) -> dict:
    key = jax.random.key(seed)
    ks = iter(jax.random.split(key, 40))
    f32 = jnp.float32

    def nrm(shape, scale):
        return jax.random.normal(next(ks), shape, f32) * scale

    def gain(shape):
        return 1.0 + nrm(shape, 0.02)

    return {
        "x": nrm((BATCH, SEQ, D_MODEL), 1.0),
        "ffn_norm": gain((DEPTH, 2, D_MODEL)),
        "ffn_w_gate": nrm((DEPTH, 2, D_MODEL, D_FF), D_MODEL ** -0.5),
        "ffn_w_up": nrm((DEPTH, 2, D_MODEL, D_FF), D_MODEL ** -0.5),
        "ffn_w_down": nrm((DEPTH, 2, D_FF, D_MODEL), D_FF ** -0.5),
        "mix_norm": gain((DEPTH, D_MODEL)),
        "a_w_in": nrm((N_EVEN, D_MODEL, EVEN_IN), D_MODEL ** -0.5),
        "a_mu": jax.random.uniform(next(ks), (N_EVEN, A_PROJ), f32, 0.0, 1.0),
        "a_w0": jax.random.uniform(next(ks), (N_EVEN, A_WIDTH), f32, -3.0, 1.0),
        "a_w2": nrm((N_EVEN, DECAY_RANK, A_WIDTH), DECAY_RANK ** -0.5),
        "a_a0": nrm((N_EVEN, A_WIDTH), 0.1),
        "a_a2": nrm((N_EVEN, ICL_RANK, A_WIDTH), ICL_RANK ** -0.5),
        "a_g2": nrm((N_EVEN, GATE_RANK, A_WIDTH), GATE_RANK ** -0.5),
        "a_k_k": 0.85 + nrm((N_EVEN, A_WIDTH), 0.05),
        "a_k_a": 1.0 + nrm((N_EVEN, A_WIDTH), 0.05),
        "a_r_k": nrm((N_EVEN, A_HEADS, HEAD_DIM), 0.1),
        "a_ln_w": gain((N_EVEN, A_WIDTH)),
        "a_ln_b": nrm((N_EVEN, A_WIDTH), 0.02),
        "b_glu_bias": nrm((N_EVEN, 2 * B_WIDTH), 0.02),
        "b_dw": nrm((N_EVEN, CONV_WIDTH, B_WIDTH), CONV_WIDTH ** -0.5),
        "b_dw_bias": nrm((N_EVEN, B_WIDTH), 0.02),
        "b_ln_w": gain((N_EVEN, B_WIDTH)),
        "b_ln_b": nrm((N_EVEN, B_WIDTH), 0.02),
        "e_w_out": nrm((N_EVEN, D_MODEL, D_MODEL), D_MODEL ** -0.5),
        "c_w_in": nrm((N_ODD, D_MODEL, ODD_IN), D_MODEL ** -0.5),
        "c_q_norm": gain((N_ODD, HEAD_DIM)),
        "c_k_norm": gain((N_ODD, HEAD_DIM)),
        "c_lq1": nrm((N_ODD, HEAD_DIM), 0.1),
        "c_lk1": nrm((N_ODD, HEAD_DIM), 0.1),
        "c_lq2": nrm((N_ODD, HEAD_DIM), 0.1),
        "c_lk2": nrm((N_ODD, HEAD_DIM), 0.1),
        "c_sub_norm": gain((N_ODD, 2 * HEAD_DIM)),
        "c_w_out": nrm((N_ODD, D_MODEL, D_MODEL), D_MODEL ** -0.5),
    }


def reference(x, ffn_norm, ffn_w_gate, ffn_w_up, ffn_w_down, mix_norm,
              a_w_in, a_mu, a_w0, a_w2, a_a0, a_a2, a_g2, a_k_k, a_k_a, a_r_k,
              a_ln_w, a_ln_b, b_glu_bias, b_dw, b_dw_bias, b_ln_w, b_ln_b, e_w_out,
              c_w_in, c_q_norm, c_k_norm, c_lq1, c_lk1, c_lq2, c_lk2, c_sub_norm,
              c_w_out):
    cos, sin = rope_tables(x.shape[1])
    for layer in range(DEPTH):
        x = x + 0.5 * swiglu(rmsnorm(x, ffn_norm[layer, 0]), ffn_w_gate[layer, 0],
                             ffn_w_up[layer, 0], ffn_w_down[layer, 0])
        h = rmsnorm(x, mix_norm[layer])
        j = layer // 2
        if layer % 2 == 0:
            x = x + rwkv_conv_mixer(h, a_w_in[j], a_mu[j], a_w0[j], a_w2[j], a_a0[j],
                                    a_a2[j], a_g2[j], a_k_k[j], a_k_a[j], a_r_k[j],
                                    a_ln_w[j], a_ln_b[j], b_glu_bias[j], b_dw[j],
                                    b_dw_bias[j], b_ln_w[j], b_ln_b[j], e_w_out[j])
        else:
            lam_init = 0.8 - 0.6 * math.exp(-0.3 * layer)
            x = x + diff_attention(h, c_w_in[j], c_q_norm[j], c_k_norm[j], c_lq1[j],
                                   c_lk1[j], c_lq2[j], c_lk2[j], c_sub_norm[j],
                                   c_w_out[j], lam_init, cos, sin)
        x = x + 0.5 * swiglu(rmsnorm(x, ffn_norm[layer, 1]), ffn_w_gate[layer, 1],
                             ffn_w_up[layer, 1], ffn_w_down[layer, 1])
    return x
```

```python
import functools
import math

import jax
import jax.numpy as jnp
from jax import lax
from jax.experimental import pallas as pl
from jax.experimental.pallas import tpu as pltpu

F32 = jnp.float32
BF16 = jnp.bfloat16

HEAD_DIM = 64
LANES = 128
CONV_WIDTH = 31
CONV_HALO = 32
DECAY_RANK = 64
ICL_RANK = 64
GATE_RANK = 128
GN_EPS = 64e-5
ROPE_THETA = 10000.0
ATTN_SCALE = HEAD_DIM ** -0.5
CHUNK = 64
VMEM_LIMIT = 56 * 1024 * 1024


def _const_spec(shape):
    return pl.BlockSpec(shape, lambda *_: (0,) * len(shape), pipeline_mode=pl.Buffered(1))


def _params(*sem):
    return pltpu.CompilerParams(dimension_semantics=sem, vmem_limit_bytes=VMEM_LIMIT)


def _dot(a, b):
    return jnp.dot(a.astype(BF16), b.astype(BF16), preferred_element_type=F32)


def _dot_nt(a, b):
    return lax.dot_general(a.astype(BF16), b.astype(BF16), (((1,), (1,)), ((), ())),
                           preferred_element_type=F32)


def _dot_tn(a, b):
    return lax.dot_general(a.astype(BF16), b.astype(BF16), (((0,), (0,)), ((), ())),
                           preferred_element_type=F32)


def _split2(x):
    hi = x.astype(BF16)
    lo = (x - hi.astype(F32)).astype(BF16)
    return hi, lo


def _split3(x):
    hi = x.astype(BF16)
    r1 = x - hi.astype(F32)
    mid = r1.astype(BF16)
    lo = (r1 - mid.astype(F32)).astype(BF16)
    return hi, mid, lo


def _dot_sel(x, sel):
    hi, lo = _split2(x)
    return (jnp.dot(hi, sel, preferred_element_type=F32)
            + jnp.dot(lo, sel, preferred_element_type=F32))


def _dot_x3(a, b_hi, b_lo):
    a_hi, a_lo = _split2(a)
    return (jnp.dot(a_hi, b_hi, preferred_element_type=F32)
            + jnp.dot(a_lo, b_hi, preferred_element_type=F32)
            + jnp.dot(a_hi, b_lo, preferred_element_type=F32))


def _rms(x, g, eps):
    return x * lax.rsqrt(jnp.mean(x * x, axis=-1, keepdims=True) + eps) * g


def _sigmoid(x):
    return 1.0 / (1.0 + jnp.exp(-x))


def _ffn_body(x_ref, g_ref, wg_ref, wu_ref, wd_ref, o_ref):
    x = x_ref[...]
    h = _rms(x, g_ref[...], 1e-6).astype(BF16)
    gate = jnp.dot(h, wg_ref[...], preferred_element_type=F32)
    up = jnp.dot(h, wu_ref[...], preferred_element_type=F32)
    act = (gate * _sigmoid(gate) * up).astype(BF16)
    o_ref[...] = x + 0.5 * jnp.dot(act, wd_ref[...], preferred_element_type=F32)


def _ffn(x2, g, wg, wu, wd, tm):
    t, d = x2.shape
    f = wg.shape[1]
    return pl.pallas_call(
        _ffn_body,
        out_shape=jax.ShapeDtypeStruct((t, d), F32),
        grid=(t // tm,),
        in_specs=[pl.BlockSpec((tm, d), lambda i: (i, 0)), _const_spec((1, d)),
                  _const_spec((d, f)), _const_spec((d, f)), _const_spec((f, d))],
        out_specs=pl.BlockSpec((tm, d), lambda i: (i, 0)),
        compiler_params=_params("parallel"),
        name="ffn",
    )(x2, g.reshape(1, d), wg, wu, wd)


def _even_in_body(a_proj, x_ref, g_ref, w_ref, za_ref, zb_ref):
    h = _rms(x_ref[...], g_ref[...], 1e-6).astype(BF16)
    z = jnp.dot(h, w_ref[...], preferred_element_type=F32)
    za_ref[...] = z[:, :a_proj]
    zb_ref[...] = z[:, a_proj:]


def _even_in(x2, g, w, a_proj, tm):
    t, d = x2.shape
    n = w.shape[1]
    return pl.pallas_call(
        functools.partial(_even_in_body, a_proj),
        out_shape=(jax.ShapeDtypeStruct((t, a_proj), F32), jax.ShapeDtypeStruct((t, n - a_proj), F32)),
        grid=(t // tm,),
        in_specs=[pl.BlockSpec((tm, d), lambda i: (i, 0)), _const_spec((1, d)), _const_spec((d, n))],
        out_specs=(pl.BlockSpec((tm, a_proj), lambda i: (i, 0)),
                   pl.BlockSpec((tm, n - a_proj), lambda i: (i, 0))),
        compiler_params=_params("parallel"),
        name="even_in",
    )(x2, g.reshape(1, d), w)


def _rwkv_prep_body(aw, za_ref, mu_ref, w0_ref, w2h_ref, w2l_ref, a0_ref, a2_ref, g2_ref, kk_ref_w, ka_ref,
                    rk_ref, sel_ref, r_out, lw_out, k_out, v_out, kk_out, a_out, g_out, bonus_out, prev_ref):
    tm = za_ref.shape[0]
    j = pl.program_id(1)

    @pl.when(j == 0)
    def _():
        prev_ref[...] = jnp.zeros_like(prev_ref)

    za = za_ref[...]
    row = lax.broadcasted_iota(jnp.int32, za.shape, 0)
    shifted = jnp.where(row == 0, prev_ref[0:1, :], pltpu.roll(za, 1, 0))
    prev_ref[0:1, :] = za[tm - 1:tm, :]
    xr = za + (shifted - za) * mu_ref[...]

    r = xr[:, 0:aw]
    k = xr[:, aw:2 * aw]
    v = xr[:, 2 * aw:3 * aw]
    lr = xr[:, 3 * aw:3 * aw + DECAY_RANK + ICL_RANK]
    gd = xr[:, 3 * aw + DECAY_RANK + ICL_RANK:]

    pre = w0_ref[...] + _dot_x3(jnp.tanh(lr), w2h_ref[...], w2l_ref[...])
    softplus_neg = jnp.maximum(-pre, 0.0) + jnp.log(1.0 + jnp.exp(-jnp.abs(pre)))
    w_log = -softplus_neg - 0.5
    lw_out[...] = -jnp.exp(w_log)

    a = _sigmoid(a0_ref[...] + jnp.dot(lr.astype(BF16), a2_ref[...], preferred_element_type=F32))
    g_out[...] = jnp.dot(_sigmoid(gd).astype(BF16), g2_ref[...], preferred_element_type=F32)

    sel = sel_ref[...]
    kk = k * kk_ref_w[...]
    kk = kk / jnp.maximum(jnp.sqrt(_dot_sel(kk * kk, sel)), 1e-12)
    kmod = k * (1.0 + (a - 1.0) * ka_ref[...])
    bonus_out[...] = _dot_sel(r * kmod * rk_ref[...], sel) * v
    r_out[...] = r
    k_out[...] = kmod
    v_out[...] = v
    kk_out[...] = kk
    a_out[...] = a


def _rwkv_prep(za, bsz, seq, aw, mu, w0, w2h, w2l, a0, a2p, g2, k_k, k_a, r_k, sel, tm):
    t, ap = za.shape
    nt = seq // tm
    row = lambda b, j: (b * nt + j, 0)
    vec = lambda n: _const_spec((1, n))
    out = jax.ShapeDtypeStruct((t, aw), F32)
    lora = DECAY_RANK + ICL_RANK
    return pl.pallas_call(
        functools.partial(_rwkv_prep_body, aw),
        out_shape=(out,) * 8,
        grid=(bsz, nt),
        in_specs=[pl.BlockSpec((tm, ap), row), vec(ap), vec(aw), _const_spec((lora, aw)), _const_spec((lora, aw)),
                  vec(aw), _const_spec((lora, aw)), _const_spec((GATE_RANK, aw)), vec(aw), vec(aw), vec(aw),
                  _const_spec((aw, aw))],
        out_specs=(pl.BlockSpec((tm, aw), row),) * 8,
        scratch_shapes=[pltpu.VMEM((8, ap), F32)],
        compiler_params=_params("parallel", "arbitrary"),
        name="rwkv_prep",
    )(za, mu, w0, w2h, w2l, a0, a2p, g2, k_k, k_a, r_k, sel)


def _conv_body(bw, zb_ref, bias_ref, dw_ref, dwb_ref, lnw_ref, lnb_ref, y_ref, buf_ref):
    tm = zb_ref.shape[0]
    j = pl.program_id(1)
    u = zb_ref[...] + bias_ref[...]
    gl = u[:, :bw] * _sigmoid(u[:, bw:])

    @pl.when(j == 0)
    def _():
        buf_ref[0:CONV_HALO, :] = jnp.zeros((CONV_HALO, bw), F32)

    @pl.when(j > 0)
    def _():
        buf_ref[0:CONV_HALO, :] = buf_ref[tm:tm + CONV_HALO, :]

    buf_ref[CONV_HALO:, :] = gl
    base = CONV_HALO - (CONV_WIDTH - 1)
    acc = jnp.zeros((tm, bw), F32)
    for tap in range(CONV_WIDTH):
        acc = acc + dw_ref[tap:tap + 1, :] * buf_ref[pl.ds(base + tap, tm), :]
    c = acc + dwb_ref[...]
    mu = jnp.mean(c, axis=-1, keepdims=True)
    var = jnp.mean(jnp.square(c - mu), axis=-1, keepdims=True)
    y = (c - mu) * lax.rsqrt(var + 1e-5) * lnw_ref[...] + lnb_ref[...]
    y_ref[...] = y * _sigmoid(y)


def _conv(zb, bsz, seq, bias, dw, dwb, lnw, lnb, tm):
    t, two_bw = zb.shape
    bw = two_bw // 2
    nt = seq // tm
    row = lambda b, j: (b * nt + j, 0)
    return pl.pallas_call(
        functools.partial(_conv_body, bw),
        out_shape=jax.ShapeDtypeStruct((t, bw), F32),
        grid=(bsz, nt),
        in_specs=[pl.BlockSpec((tm, two_bw), row), _const_spec((1, two_bw)), _const_spec((CONV_HALO, bw)),
                  _const_spec((1, bw)), _const_spec((1, bw)), _const_spec((1, bw))],
        out_specs=pl.BlockSpec((tm, bw), row),
        scratch_shapes=[pltpu.VMEM((tm + CONV_HALO, bw), F32)],
        compiler_params=_params("parallel", "arbitrary"),
        name="conv_module",
    )(zb, bias, dw, dwb, lnw, lnb)


def _wkv_body(r_ref, lw_ref, k_ref, v_ref, kk_ref, a_ref, y_ref, state_ref):
    ch, width = r_ref.shape
    n_pairs = width // LANES
    c = pl.program_id(1)

    @pl.when(c == 0)
    def _():
        state_ref[...] = jnp.zeros_like(state_ref)

    ti = lax.broadcasted_iota(jnp.int32, (ch, ch), 0)
    tj = lax.broadcasted_iota(jnp.int32, (ch, ch), 1)
    tri = (ti >= tj).astype(BF16)
    lw = lw_ref[...]
    lg = sum(jnp.dot(tri, part, preferred_element_type=F32) for part in _split3(lw))
    lg_last = lg[ch - 1:ch, :]
    e_pos = jnp.exp(lg)
    e_neg = jnp.exp(-lg)
    e_prev = jnp.exp(lg - lw)
    e_tail = jnp.exp(lg_last - lg)
    kk = kk_ref[...]
    kka = kk * a_ref[...]
    kmod = k_ref[...]
    r_t = (r_ref[...] * e_pos).astype(BF16)
    a_t = (-kk * e_prev).astype(BF16)
    b_t = (kka * e_neg).astype(BF16)
    k_t = (kmod * e_neg).astype(BF16)
    b_h = (kka * e_tail).astype(BF16)
    k_h = (kmod * e_tail).astype(BF16)
    v_b = v_ref[...].astype(BF16)
    g_last = jnp.exp(lg_last)

    lane = lax.broadcasted_iota(jnp.int32, (ch, LANES), 1)
    head0 = lane < HEAD_DIM

    def stack(x):
        zero = jnp.zeros_like(x)
        return jnp.concatenate([jnp.where(head0, x, zero), jnp.where(head0, zero, x)], axis=0)

    n = 2 * ch
    ri = lax.broadcasted_iota(jnp.int32, (n, n), 0) & (ch - 1)
    ci = lax.broadcasted_iota(jnp.int32, (n, n), 1) & (ch - 1)
    strict = ri > ci
    incl = ri >= ci
    eye = (lax.broadcasted_iota(jnp.int32, (n, n), 0) == lax.broadcasted_iota(jnp.int32, (n, n), 1)).astype(F32)

    for p in range(n_pairs):
        sl = slice(p * LANES, (p + 1) * LANES)
        rd, ad, bd, kd = stack(r_t[:, sl]), stack(a_t[:, sl]), stack(b_t[:, sl]), stack(k_t[:, sl])
        bhd, khd, vd = stack(b_h[:, sl]), stack(k_h[:, sl]), stack(v_b[:, sl])

        lhs = jnp.concatenate([ad, rd], axis=0)
        ob = _dot_nt(lhs, bd)
        ok = _dot_nt(lhs, kd)
        a_ab = jnp.where(strict, ob[:n], 0.0)
        a_rb = jnp.where(incl, ob[n:], 0.0)
        a_ak = jnp.where(strict, ok[:n], 0.0)
        a_rk = jnp.where(incl, ok[n:], 0.0)

        pw = a_ab
        tinv = eye + a_ab
        for _ in range(int(math.log2(ch)) - 1):
            pw = _dot(pw, pw)
            tinv = tinv + _dot(tinv, pw)

        w1 = _dot(tinv, ad)
        w2 = _dot(tinv, _dot(a_ak, vd))
        s_old = state_ref[p]
        s_b = s_old.astype(BF16)
        u = _dot_nt(w1, s_b) + w2
        u_b = u.astype(BF16)
        y = _dot_nt(rd, s_b) + _dot(a_rb, u_b) + _dot(a_rk, vd)
        y_ref[:, sl] = y[:ch] + y[ch:]
        state_ref[p] = s_old * g_last[:, sl] + _dot_tn(u_b, bhd) + _dot_tn(vd, khd)


def _wkv(r, lw, k, v, kk, a, bsz, seq):
    t, width = r.shape
    nc = seq // CHUNK
    row = lambda b, c: (b * nc + c, 0)
    spec = pl.BlockSpec((CHUNK, width), row)
    return pl.pallas_call(
        _wkv_body,
        out_shape=jax.ShapeDtypeStruct((t, width), F32),
        grid=(bsz, nc),
        in_specs=[spec] * 6,
        out_specs=spec,
        scratch_shapes=[pltpu.VMEM((width // LANES, LANES, LANES), F32)],
        compiler_params=_params("parallel", "arbitrary"),
        name="wkv7_chunk",
    )(r, lw, k, v, kk, a)


def _even_out_body(aw, y_ref, bonus_ref, g_ref, yb_ref, x_ref, lnw_ref, lnb_ref, sel_ref, w_ref, o_ref):
    sel = sel_ref[...]
    y = y_ref[...]
    inv_n = 1.0 / HEAD_DIM
    d = y - _dot_sel(y, sel) * inv_n
    var = _dot_sel(d * d, sel) * inv_n
    yn = d * lax.rsqrt(var + GN_EPS) * lnw_ref[...] + lnb_ref[...]
    ya = ((yn + bonus_ref[...]) * g_ref[...]).astype(BF16)
    o_ref[...] = (x_ref[...]
                  + jnp.dot(ya, w_ref[0:aw, :], preferred_element_type=F32)
                  + jnp.dot(yb_ref[...].astype(BF16), w_ref[aw:, :], preferred_element_type=F32))


def _even_out(y, bonus, g, yb, x2, lnw, lnb, sel, w, tm):
    t, d = x2.shape
    aw = y.shape[1]
    bw = yb.shape[1]
    row = lambda i: (i, 0)
    return pl.pallas_call(
        functools.partial(_even_out_body, aw),
        out_shape=jax.ShapeDtypeStruct((t, d), F32),
        grid=(t // tm,),
        in_specs=[pl.BlockSpec((tm, aw), row), pl.BlockSpec((tm, aw), row), pl.BlockSpec((tm, aw), row),
                  pl.BlockSpec((tm, bw), row), pl.BlockSpec((tm, d), row), _const_spec((1, aw)),
                  _const_spec((1, aw)), _const_spec((aw, aw)), _const_spec((d, d))],
        out_specs=pl.BlockSpec((tm, d), row),
        compiler_params=_params("parallel"),
        name="even_out",
    )(y, bonus, g, yb, x2, lnw, lnb, sel, w)


def _odd_in_body(d, x_ref, g_ref, w_ref, qn_ref, kn_ref, cos_ref, sin_ref, sel_ref, q_ref, k_ref, v_ref):
    h = _rms(x_ref[...], g_ref[...], 1e-6).astype(BF16)
    z = jnp.dot(h, w_ref[...], preferred_element_type=F32)
    tm = z.shape[0]
    sel = sel_ref[...]
    cos = cos_ref[...]
    sin = sin_ref[...]
    lane = lax.broadcasted_iota(jnp.int32, (tm, LANES), 1)
    first_half = (lane & (HEAD_DIM - 1)) < HEAD_DIM // 2
    for blk in range(d // LANES):
        sl = slice(blk * LANES, (blk + 1) * LANES)
        for off, norm_ref, out_ref, scale in ((0, qn_ref, q_ref, ATTN_SCALE), (d, kn_ref, k_ref, 1.0)):
            xq = z[:, off + blk * LANES:off + (blk + 1) * LANES]
            ms = _dot_sel(xq * xq, sel) * (1.0 / HEAD_DIM)
            xn = xq * lax.rsqrt(ms + 1e-6) * norm_ref[:, sl]
            rot = jnp.where(first_half, pltpu.roll(xn, LANES - HEAD_DIM // 2, 1), pltpu.roll(xn, HEAD_DIM // 2, 1))
            out_ref[:, sl] = ((xn * cos + rot * sin) * scale).astype(BF16)
    v_ref[...] = z[:, 2 * d:].astype(BF16)


def _odd_in(x2, bsz, seq, g, w, qn, kn, cos, sin, sel, tm):
    t, d = x2.shape
    nt = seq // tm
    row = lambda b, j: (b * nt + j, 0)
    pos = lambda b, j: (j, 0)
    out = jax.ShapeDtypeStruct((t, d), BF16)
    return pl.pallas_call(
        functools.partial(_odd_in_body, d),
        out_shape=(out, out, out),
        grid=(bsz, nt),
        in_specs=[pl.BlockSpec((tm, d), row), _const_spec((1, d)), _const_spec((d, 3 * d)), _const_spec((1, d)),
                  _const_spec((1, d)), pl.BlockSpec((tm, LANES), pos), pl.BlockSpec((tm, LANES), pos),
                  _const_spec((LANES, LANES))],
        out_specs=(pl.BlockSpec((tm, d), row),) * 3,
        compiler_params=_params("parallel", "parallel"),
        name="odd_in",
    )(x2, g.reshape(1, d), w, qn, kn, cos, sin, sel)


def _attn_body(lam_init, q_ref, k_ref, v_ref, lam_ref, sn_ref, o_ref):
    tq = q_ref.shape[0]
    tk = tq
    i = pl.program_id(2)
    q = q_ref[...]
    lane = lax.broadcasted_iota(jnp.int32, q.shape, 1)
    comp0 = lane < HEAD_DIM
    zero = jnp.zeros_like(q)
    qd = jnp.concatenate([jnp.where(comp0, q, zero), jnp.where(comp0, zero, q)], axis=0)

    def block(j, carry, masked):
        m, l, acc = carry
        start = pl.multiple_of(j * tk, tk)
        kj = k_ref[pl.ds(start, tk), :]
        vj = v_ref[pl.ds(start, tk), :]
        s = lax.dot_general(qd, kj, (((1,), (1,)), ((), ())), preferred_element_type=F32)
        if masked:
            qi = lax.broadcasted_iota(jnp.int32, s.shape, 0) & (tq - 1)
            ki = lax.broadcasted_iota(jnp.int32, s.shape, 1)
            s = jnp.where(ki <= qi, s, -jnp.inf)
        m_new = jnp.maximum(m, jnp.max(s, axis=-1, keepdims=True))
        alpha = jnp.exp(m - m_new)
        p = jnp.exp(s - m_new)
        l = alpha * l + jnp.sum(p, axis=-1, keepdims=True)
        acc = alpha * acc + jnp.dot(p.astype(BF16), vj, preferred_element_type=F32)
        return m_new, l, acc

    init = (jnp.full((2 * tq, 1), -1e30, F32), jnp.zeros((2 * tq, 1), F32), jnp.zeros((2 * tq, LANES), F32))
    carry = lax.fori_loop(0, i, lambda j, c: block(j, c, False), init)
    _, l, acc = block(i, carry, True)
    o = acc / l
    lv = lam_ref[...]
    lam = (jnp.exp(jnp.sum(lv[0:1] * lv[1:2], axis=-1, keepdims=True))
           - jnp.exp(jnp.sum(lv[2:3] * lv[3:4], axis=-1, keepdims=True)) + lam_init)
    out = o[:tq] - lam * o[tq:]
    out = _rms(out, sn_ref[...], 1e-5) * (1.0 - lam_init)
    o_ref[...] = out.astype(BF16)


def _attn(q3, k3, v3, lam_vecs, sub_norm, lam_init, tq):
    bsz, seq, d = q3.shape
    heads = d // LANES
    return pl.pallas_call(
        functools.partial(_attn_body, lam_init),
        out_shape=jax.ShapeDtypeStruct((bsz, seq, d), BF16),
        grid=(bsz, heads, seq // tq),
        in_specs=[pl.BlockSpec((None, tq, LANES), lambda b, h, i: (b, i, h)),
                  pl.BlockSpec((None, seq, LANES), lambda b, h, i: (b, 0, h)),
                  pl.BlockSpec((None, seq, LANES), lambda b, h, i: (b, 0, h)),
                  _const_spec((4, HEAD_DIM)), _const_spec((1, LANES))],
        out_specs=pl.BlockSpec((None, tq, LANES), lambda b, h, i: (b, i, h)),
        compiler_params=_params("parallel", "parallel", "arbitrary"),
        name="diff_attn",
    )(q3, k3, v3, lam_vecs, sub_norm)


def _proj_res_body(o_ref, x_ref, w_ref, out_ref):
    out_ref[...] = x_ref[...] + jnp.dot(o_ref[...], w_ref[...], preferred_element_type=F32)


def _proj_res(o2, x2, w, tm):
    t, d = x2.shape
    row = lambda i: (i, 0)
    return pl.pallas_call(
        _proj_res_body,
        out_shape=jax.ShapeDtypeStruct((t, d), F32),
        grid=(t // tm,),
        in_specs=[pl.BlockSpec((tm, d), row), pl.BlockSpec((tm, d), row), _const_spec((d, d))],
        out_specs=pl.BlockSpec((tm, d), row),
        compiler_params=_params("parallel"),
        name="proj_res",
    )(o2, x2, w)


def _block_ones(n, blk):
    idx = jnp.arange(n) // blk
    return (idx[:, None] == idx[None, :]).astype(BF16)


def _rope_tables(seq):
    inv = 1.0 / (ROPE_THETA ** (jnp.arange(0, HEAD_DIM, 2, dtype=F32) / HEAD_DIM))
    ang = jnp.arange(seq, dtype=F32)[:, None] * inv[None, :]
    cos, sin = jnp.cos(ang), jnp.sin(ang)
    reps = LANES // HEAD_DIM
    return jnp.tile(jnp.concatenate([cos, cos], -1), (1, reps)), jnp.tile(jnp.concatenate([-sin, sin], -1), (1, reps))


def _row_tile(seq):
    return min(256, seq)


def kernel(x, ffn_norm, ffn_w_gate, ffn_w_up, ffn_w_down, mix_norm, a_w_in, a_mu, a_w0, a_w2, a_a0, a_a2, a_g2, a_k_k, a_k_a, a_r_k, a_ln_w, a_ln_b, b_glu_bias, b_dw, b_dw_bias, b_ln_w, b_ln_b, e_w_out, c_w_in, c_q_norm, c_k_norm, c_lq1, c_lk1, c_lq2, c_lk2, c_sub_norm, c_w_out):
    bsz, seq, d = x.shape
    depth = ffn_norm.shape[0]
    t = bsz * seq
    tm = _row_tile(seq)
    x2 = x.reshape(t, d)
    wg, wu, wd = ffn_w_gate.astype(BF16), ffn_w_up.astype(BF16), ffn_w_down.astype(BF16)
    cos, sin = _rope_tables(seq)

    def ffn(x2, layer, which):
        return _ffn(x2, ffn_norm[layer, which], wg[layer, which], wu[layer, which], wd[layer, which], tm)

    for layer in range(depth):
        x2 = ffn(x2, layer, 0)
        j = layer // 2
        if layer % 2 == 0:
            aw = a_w0.shape[1]
            a_proj = a_mu.shape[1]
            row = lambda v: v.reshape(1, -1)
            sel = _block_ones(aw, HEAD_DIM)
            za, zb = _even_in(x2, mix_norm[layer], a_w_in[j].astype(BF16), a_proj, tm)
            w2p = jnp.pad(a_w2[j], ((0, ICL_RANK), (0, 0)))
            w2h = w2p.astype(BF16)
            w2l = (w2p - w2h.astype(F32)).astype(BF16)
            a2p = jnp.pad(a_a2[j], ((DECAY_RANK, 0), (0, 0))).astype(BF16)
            r, lw, k, v, kk, a, g, bonus = _rwkv_prep(
                za, bsz, seq, aw, row(a_mu[j]), row(a_w0[j]), w2h, w2l, row(a_a0[j]), a2p, a_g2[j].astype(BF16),
                row(a_k_k[j]), row(a_k_a[j]), row(a_r_k[j]), sel, tm)
            yb = _conv(zb, bsz, seq, row(b_glu_bias[j]), jnp.pad(b_dw[j], ((0, CONV_HALO - CONV_WIDTH), (0, 0))),
                       row(b_dw_bias[j]), row(b_ln_w[j]), row(b_ln_b[j]), tm)
            y = _wkv(r, lw, k, v, kk, a, bsz, seq)
            x2 = _even_out(y, bonus, g, yb, x2, row(a_ln_w[j]), row(a_ln_b[j]), sel, e_w_out[j].astype(BF16), tm)
        else:
            lam_init = 0.8 - 0.6 * math.exp(-0.3 * layer)
            reps = d // HEAD_DIM
            q, k, v = _odd_in(x2, bsz, seq, mix_norm[layer], c_w_in[j].astype(BF16),
                              jnp.tile(c_q_norm[j], reps).reshape(1, d), jnp.tile(c_k_norm[j], reps).reshape(1, d),
                              cos, sin, _block_ones(LANES, HEAD_DIM), tm)
            lam_vecs = jnp.stack([c_lq1[j], c_lk1[j], c_lq2[j], c_lk2[j]])
            o = _attn(q.reshape(bsz, seq, d), k.reshape(bsz, seq, d), v.reshape(bsz, seq, d), lam_vecs,
                      c_sub_norm[j].reshape(1, LANES), lam_init, tm)
            x2 = _proj_res(o.reshape(t, d), x2, c_w_out[j].astype(BF16), tm)
        x2 = ffn(x2, layer, 1)
    return x2.reshape(bsz, seq, d)
```

```python
import functools
import math

import jax
import jax.numpy as jnp
from jax import lax
from jax.experimental import pallas as pl
from jax.experimental.pallas import tpu as pltpu

F32 = jnp.float32
BF16 = jnp.bfloat16

HEAD_DIM = 64
LANES = 128
CONV_WIDTH = 31
CONV_HALO = 32
DECAY_RANK = 64
ICL_RANK = 64
GATE_RANK = 128
GN_EPS = 64e-5
ROPE_THETA = 10000.0
ATTN_SCALE = HEAD_DIM ** -0.5
CHUNK = 64
VMEM_LIMIT = 56 * 1024 * 1024


def _const_spec(shape):
    return pl.BlockSpec(shape, lambda *_: (0,) * len(shape), pipeline_mode=pl.Buffered(1))


def _params(*sem):
    return pltpu.CompilerParams(dimension_semantics=sem, vmem_limit_bytes=VMEM_LIMIT)


def _dot(a, b):
    return jnp.dot(a.astype(BF16), b.astype(BF16), preferred_element_type=F32)


def _dot_nt(a, b):
    return lax.dot_general(a.astype(BF16), b.astype(BF16), (((1,), (1,)), ((), ())),
                           preferred_element_type=F32)


def _dot_tn(a, b):
    return lax.dot_general(a.astype(BF16), b.astype(BF16), (((0,), (0,)), ((), ())),
                           preferred_element_type=F32)


def _split2(x):
    hi = x.astype(BF16)
    lo = (x - hi.astype(F32)).astype(BF16)
    return hi, lo


def _split3(x):
    hi = x.astype(BF16)
    r1 = x - hi.astype(F32)
    mid = r1.astype(BF16)
    lo = (r1 - mid.astype(F32)).astype(BF16)
    return hi, mid, lo


def _dot_sel(x, sel):
    hi, lo = _split2(x)
    return (jnp.dot(hi, sel, preferred_element_type=F32)
            + jnp.dot(lo, sel, preferred_element_type=F32))


def _dot_x3(a, b_hi, b_lo):
    a_hi, a_lo = _split2(a)
    return (jnp.dot(a_hi, b_hi, preferred_element_type=F32)
            + jnp.dot(a_lo, b_hi, preferred_element_type=F32)
            + jnp.dot(a_hi, b_lo, preferred_element_type=F32))


def _rms(x, g, eps):
    return x * lax.rsqrt(jnp.mean(x * x, axis=-1, keepdims=True) + eps) * g


def _sigmoid(x):
    return 1.0 / (1.0 + jnp.exp(-x))


def _ffn_body(x_ref, g_ref, wg_ref, wu_ref, wd_ref, o_ref):
    x = x_ref[...]
    h = _rms(x, g_ref[...], 1e-6).astype(BF16)
    gate = jnp.dot(h, wg_ref[...], preferred_element_type=F32)
    up = jnp.dot(h, wu_ref[...], preferred_element_type=F32)
    act = (gate * _sigmoid(gate) * up).astype(BF16)
    o_ref[...] = x + 0.5 * jnp.dot(act, wd_ref[...], preferred_element_type=F32)


def _ffn(x2, g, wg, wu, wd, tm):
    t, d = x2.shape
    f = wg.shape[1]
    return pl.pallas_call(
        _ffn_body,
        out_shape=jax.ShapeDtypeStruct((t, d), F32),
        grid=(t // tm,),
        in_specs=[pl.BlockSpec((tm, d), lambda i: (i, 0)), _const_spec((1, d)),
                  _const_spec((d, f)), _const_spec((d, f)), _const_spec((f, d))],
        out_specs=pl.BlockSpec((tm, d), lambda i: (i, 0)),
        compiler_params=_params("parallel"),
        name="ffn",
    )(x2, g.reshape(1, d), wg, wu, wd)


def _even_in_body(a_proj, x_ref, g_ref, w_ref, za_ref, zb_ref):
    h = _rms(x_ref[...], g_ref[...], 1e-6).astype(BF16)
    z = jnp.dot(h, w_ref[...], preferred_element_type=F32)
    za_ref[...] = z[:, :a_proj]
    zb_ref[...] = z[:, a_proj:]


def _even_in(x2, g, w, a_proj, tm):
    t, d = x2.shape
    n = w.shape[1]
    return pl.pallas_call(
        functools.partial(_even_in_body, a_proj),
        out_shape=(jax.ShapeDtypeStruct((t, a_proj), F32), jax.ShapeDtypeStruct((t, n - a_proj), F32)),
        grid=(t // tm,),
        in_specs=[pl.BlockSpec((tm, d), lambda i: (i, 0)), _const_spec((1, d)), _const_spec((d, n))],
        out_specs=(pl.BlockSpec((tm, a_proj), lambda i: (i, 0)),
                   pl.BlockSpec((tm, n - a_proj), lambda i: (i, 0))),
        compiler_params=_params("parallel"),
        name="even_in",
    )(x2, g.reshape(1, d), w)


def _rwkv_prep_body(aw, za_ref, mu_ref, w0_ref, w2h_ref, w2l_ref, a0_ref, a2_ref, g2_ref, kk_ref_w, ka_ref,
                    rk_ref, sel_ref, r_out, lw_out, k_out, v_out, kk_out, a_out, g_out, bonus_out, prev_ref):
    tm = za_ref.shape[0]
    j = pl.program_id(1)

    @pl.when(j == 0)
    def _():
        prev_ref[...] = jnp.zeros_like(prev_ref)

    za = za_ref[...]
    row = lax.broadcasted_iota(jnp.int32, za.shape, 0)
    shifted = jnp.where(row == 0, prev_ref[0:1, :], pltpu.roll(za, 1, 0))
    prev_ref[0:1, :] = za[tm - 1:tm, :]
    xr = za + (shifted - za) * mu_ref[...]

    r = xr[:, 0:aw]
    k = xr[:, aw:2 * aw]
    v = xr[:, 2 * aw:3 * aw]
    lr = xr[:, 3 * aw:3 * aw + DECAY_RANK + ICL_RANK]
    gd = xr[:, 3 * aw + DECAY_RANK + ICL_RANK:]

    pre = w0_ref[...] + _dot_x3(jnp.tanh(lr), w2h_ref[...], w2l_ref[...])
    softplus_neg = jnp.maximum(-pre, 0.0) + jnp.log(1.0 + jnp.exp(-jnp.abs(pre)))
    w_log = -softplus_neg - 0.5
    lw_out[...] = -jnp.exp(w_log)

    a = _sigmoid(a0_ref[...] + jnp.dot(lr.astype(BF16), a2_ref[...], preferred_element_type=F32))
    g_out[...] = jnp.dot(_sigmoid(gd).astype(BF16), g2_ref[...], preferred_element_type=F32)

    sel = sel_ref[...]
    kk = k * kk_ref_w[...]
    kk = kk / jnp.maximum(jnp.sqrt(_dot_sel(kk * kk, sel)), 1e-12)
    kmod = k * (1.0 + (a - 1.0) * ka_ref[...])
    bonus_out[...] = _dot_sel(r * kmod * rk_ref[...], sel) * v
    r_out[...] = r
    k_out[...] = kmod
    v_out[...] = v
    kk_out[...] = kk
    a_out[...] = a


def _rwkv_prep(za, bsz, seq, aw, mu, w0, w2h, w2l, a0, a2p, g2, k_k, k_a, r_k, sel, tm):
    t, ap = za.shape
    nt = seq // tm
    row = lambda b, j: (b * nt + j, 0)
    vec = lambda n: _const_spec((1, n))
    out = jax.ShapeDtypeStruct((t, aw), F32)
    lora = DECAY_RANK + ICL_RANK
    return pl.pallas_call(
        functools.partial(_rwkv_prep_body, aw),
        out_shape=(out,) * 8,
        grid=(bsz, nt),
        in_specs=[pl.BlockSpec((tm, ap), row), vec(ap), vec(aw), _const_spec((lora, aw)), _const_spec((lora, aw)),
                  vec(aw), _const_spec((lora, aw)), _const_spec((GATE_RANK, aw)), vec(aw), vec(aw), vec(aw),
                  _const_spec((aw, aw))],
        out_specs=(pl.BlockSpec((tm, aw), row),) * 8,
        scratch_shapes=[pltpu.VMEM((8, ap), F32)],
        compiler_params=_params("parallel", "arbitrary"),
        name="rwkv_prep",
    )(za, mu, w0, w2h, w2l, a0, a2p, g2, k_k, k_a, r_k, sel)


def _conv_body(bw, zb_ref, bias_ref, dw_ref, dwb_ref, lnw_ref, lnb_ref, y_ref, buf_ref):
    tm = zb_ref.shape[0]
    j = pl.program_id(1)
    u = zb_ref[...] + bias_ref[...]
    gl = u[:, :bw] * _sigmoid(u[:, bw:])

    @pl.when(j == 0)
    def _():
        buf_ref[0:CONV_HALO, :] = jnp.zeros((CONV_HALO, bw), F32)

    @pl.when(j > 0)
    def _():
        buf_ref[0:CONV_HALO, :] = buf_ref[tm:tm + CONV_HALO, :]

    buf_ref[CONV_HALO:, :] = gl
    base = CONV_HALO - (CONV_WIDTH - 1)
    acc = jnp.zeros((tm, bw), F32)
    for tap in range(CONV_WIDTH):
        acc = acc + dw_ref[tap:tap + 1, :] * buf_ref[pl.ds(base + tap, tm), :]
    c = acc + dwb_ref[...]
    mu = jnp.mean(c, axis=-1, keepdims=True)
    var = jnp.mean(jnp.square(c - mu), axis=-1, keepdims=True)
    y = (c - mu) * lax.rsqrt(var + 1e-5) * lnw_ref[...] + lnb_ref[...]
    y_ref[...] = y * _sigmoid(y)


def _conv(zb, bsz, seq, bias, dw, dwb, lnw, lnb, tm):
    t, two_bw = zb.shape
    bw = two_bw // 2
    nt = seq // tm
    row = lambda b, j: (b * nt + j, 0)
    return pl.pallas_call(
        functools.partial(_conv_body, bw),
        out_shape=jax.ShapeDtypeStruct((t, bw), F32),
        grid=(bsz, nt),
        in_specs=[pl.BlockSpec((tm, two_bw), row), _const_spec((1, two_bw)), _const_spec((CONV_HALO, bw)),
                  _const_spec((1, bw)), _const_spec((1, bw)), _const_spec((1, bw))],
        out_specs=pl.BlockSpec((tm, bw), row),
        scratch_shapes=[pltpu.VMEM((tm + CONV_HALO, bw), F32)],
        compiler_params=_params("parallel", "arbitrary"),
        name="conv_module",
    )(zb, bias, dw, dwb, lnw, lnb)


def _wkv_body(r_ref, lw_ref, k_ref, v_ref, kk_ref, a_ref, y_ref, state_ref):
    ch, width = r_ref.shape
    n_pairs = width // LANES
    c = pl.program_id(1)

    @pl.when(c == 0)
    def _():
        state_ref[...] = jnp.zeros_like(state_ref)

    ti = lax.broadcasted_iota(jnp.int32, (ch, ch), 0)
    tj = lax.broadcasted_iota(jnp.int32, (ch, ch), 1)
    tri = (ti >= tj).astype(BF16)
    lw = lw_ref[...]
    lg = sum(jnp.dot(tri, part, preferred_element_type=F32) for part in _split3(lw))
    lg_last = lg[ch - 1:ch, :]
    e_pos = jnp.exp(lg)
    e_neg = jnp.exp(-lg)
    e_prev = jnp.exp(lg - lw)
    e_tail = jnp.exp(lg_last - lg)
    kk = kk_ref[...]
    kka = kk * a_ref[...]
    kmod = k_ref[...]
    r_t = (r_ref[...] * e_pos).astype(BF16)
    a_t = (-kk * e_prev).astype(BF16)
    b_t = (kka * e_neg).astype(BF16)
    k_t = (kmod * e_neg).astype(BF16)
    b_h = (kka * e_tail).astype(BF16)
    k_h = (kmod * e_tail).astype(BF16)
    v_b = v_ref[...].astype(BF16)
    g_last = jnp.exp(lg_last)

    lane = lax.broadcasted_iota(jnp.int32, (ch, LANES), 1)
    head0 = lane < HEAD_DIM

    def stack(x):
        zero = jnp.zeros_like(x)
        return jnp.concatenate([jnp.where(head0, x, zero), jnp.where(head0, zero, x)], axis=0)

    n = 2 * ch
    ri = lax.broadcasted_iota(jnp.int32, (n, n), 0) & (ch - 1)
    ci = lax.broadcasted_iota(jnp.int32, (n, n), 1) & (ch - 1)
    strict = ri > ci
    incl = ri >= ci
    eye = (lax.broadcasted_iota(jnp.int32, (n, n), 0) == lax.broadcasted_iota(jnp.int32, (n, n), 1)).astype(F32)

    for p in range(n_pairs):
        sl = slice(p * LANES, (p + 1) * LANES)
        rd, ad, bd, kd = stack(r_t[:, sl]), stack(a_t[:, sl]), stack(b_t[:, sl]), stack(k_t[:, sl])
        bhd, khd, vd = stack(b_h[:, sl]), stack(k_h[:, sl]), stack(v_b[:, sl])

        lhs = jnp.concatenate([ad, rd], axis=0)
        ob = _dot_nt(lhs, bd)
        ok = _dot_nt(lhs, kd)
        a_ab = jnp.where(strict, ob[:n], 0.0)
        a_rb = jnp.where(incl, ob[n:], 0.0)
        a_ak = jnp.where(strict, ok[:n], 0.0)
        a_rk = jnp.where(incl, ok[n:], 0.0)

        pw = a_ab
        tinv = eye + a_ab
        for _ in range(int(math.log2(ch)) - 1):
            pw = _dot(pw, pw)
            tinv = tinv + _dot(tinv, pw)

        w1 = _dot(tinv, ad)
        w2 = _dot(tinv, _dot(a_ak, vd))
        s_old = state_ref[p]
        s_b = s_old.astype(BF16)
        u = _dot_nt(w1, s_b) + w2
        u_b = u.astype(BF16)
        y = _dot_nt(rd, s_b) + _dot(a_rb, u_b) + _dot(a_rk, vd)
        y_ref[:, sl] = y[:ch] + y[ch:]
        state_ref[p] = s_old * g_last[:, sl] + _dot_tn(u_b, bhd) + _dot_tn(vd, khd)


def _wkv(r, lw, k, v, kk, a, bsz, seq):
    t, width = r.shape
    nc = seq // CHUNK
    row = lambda b, c: (b * nc + c, 0)
    spec = pl.BlockSpec((CHUNK, width), row)
    return pl.pallas_call(
        _wkv_body,
        out_shape=jax.ShapeDtypeStruct((t, width), F32),
        grid=(bsz, nc),
        in_specs=[spec] * 6,
        out_specs=spec,
        scratch_shapes=[pltpu.VMEM((width // LANES, LANES, LANES), F32)],
        compiler_params=_params("parallel", "arbitrary"),
        name="wkv7_chunk",
    )(r, lw, k, v, kk, a)


def _even_out_body(aw, y_ref, bonus_ref, g_ref, yb_ref, x_ref, lnw_ref, lnb_ref, sel_ref, w_ref, o_ref):
    sel = sel_ref[...]
    y = y_ref[...]
    inv_n = 1.0 / HEAD_DIM
    d = y - _dot_sel(y, sel) * inv_n
    var = _dot_sel(d * d, sel) * inv_n
    yn = d * lax.rsqrt(var + GN_EPS) * lnw_ref[...] + lnb_ref[...]
    ya = ((yn + bonus_ref[...]) * g_ref[...]).astype(BF16)
    o_ref[...] = (x_ref[...]
                  + jnp.dot(ya, w_ref[0:aw, :], preferred_element_type=F32)
                  + jnp.dot(yb_ref[...].astype(BF16), w_ref[aw:, :], preferred_element_type=F32))


def _even_out(y, bonus, g, yb, x2, lnw, lnb, sel, w, tm):
    t, d = x2.shape
    aw = y.shape[1]
    bw = yb.shape[1]
    row = lambda i: (i, 0)
    return pl.pallas_call(
        functools.partial(_even_out_body, aw),
        out_shape=jax.ShapeDtypeStruct((t, d), F32),
        grid=(t // tm,),
        in_specs=[pl.BlockSpec((tm, aw), row), pl.BlockSpec((tm, aw), row), pl.BlockSpec((tm, aw), row),
                  pl.BlockSpec((tm, bw), row), pl.BlockSpec((tm, d), row), _const_spec((1, aw)),
                  _const_spec((1, aw)), _const_spec((aw, aw)), _const_spec((d, d))],
        out_specs=pl.BlockSpec((tm, d), row),
        compiler_params=_params("parallel"),
        name="even_out",
    )(y, bonus, g, yb, x2, lnw, lnb, sel, w)


def _odd_in_body(d, x_ref, g_ref, w_ref, qn_ref, kn_ref, cos_ref, sin_ref, sel_ref, q_ref, k_ref, v_ref):
    h = _rms(x_ref[...], g_ref[...], 1e-6).astype(BF16)
    z = jnp.dot(h, w_ref[...], preferred_element_type=F32)
    tm = z.shape[0]
    sel = sel_ref[...]
    cos = cos_ref[...]
    sin = sin_ref[...]
    lane = lax.broadcasted_iota(jnp.int32, (tm, LANES), 1)
    first_half = (lane & (HEAD_DIM - 1)) < HEAD_DIM // 2
    for blk in range(d // LANES):
        sl = slice(blk * LANES, (blk + 1) * LANES)
        for off, norm_ref, out_ref, scale in ((0, qn_ref, q_ref, ATTN_SCALE), (d, kn_ref, k_ref, 1.0)):
            xq = z[:, off + blk * LANES:off + (blk + 1) * LANES]
            ms = _dot_sel(xq * xq, sel) * (1.0 / HEAD_DIM)
            xn = xq * lax.rsqrt(ms + 1e-6) * norm_ref[:, sl]
            rot = jnp.where(first_half, pltpu.roll(xn, LANES - HEAD_DIM // 2, 1), pltpu.roll(xn, HEAD_DIM // 2, 1))
            out_ref[:, sl] = ((xn * cos + rot * sin) * scale).astype(BF16)
    v_ref[...] = z[:, 2 * d:].astype(BF16)


def _odd_in(x2, bsz, seq, g, w, qn, kn, cos, sin, sel, tm):
    t, d = x2.shape
    nt = seq // tm
    row = lambda b, j: (b * nt + j, 0)
    pos = lambda b, j: (j, 0)
    out = jax.ShapeDtypeStruct((t, d), BF16)
    return pl.pallas_call(
        functools.partial(_odd_in_body, d),
        out_shape=(out, out, out),
        grid=(bsz, nt),
        in_specs=[pl.BlockSpec((tm, d), row), _const_spec((1, d)), _const_spec((d, 3 * d)), _const_spec((1, d)),
                  _const_spec((1, d)), pl.BlockSpec((tm, LANES), pos), pl.BlockSpec((tm, LANES), pos),
                  _const_spec((LANES, LANES))],
        out_specs=(pl.BlockSpec((tm, d), row),) * 3,
        compiler_params=_params("parallel", "parallel"),
        name="odd_in",
    )(x2, g.reshape(1, d), w, qn, kn, cos, sin, sel)


ATTN_TQ = 512
ATTN_TK = 256
SUM_ROWS = 16


def _attn_body(lam_init, q_ref, k_ref, vt_ref, lam_ref, sn_ref, o_ref, qd_buf, s_buf, p_buf, al_buf, m_ref, acc_ref):
    seq = q_ref.shape[0]
    tk = vt_ref.shape[-1]
    tq = qd_buf.shape[1] // 2
    per_q = tq // tk
    ones = jnp.ones((SUM_ROWS, tk), BF16)
    lv = lam_ref[...]
    lam = (jnp.exp(jnp.sum(lv[0:1] * lv[1:2], axis=-1, keepdims=True))
           - jnp.exp(jnp.sum(lv[2:3] * lv[3:4], axis=-1, keepdims=True)) + lam_init)
    lane = lax.broadcasted_iota(jnp.int32, (tq, LANES), 1)
    comp0 = lane < HEAD_DIM

    def scores(par, j):
        start = j * tk if isinstance(j, int) else pl.multiple_of(j * tk, tk)
        kj = k_ref[pl.ds(start, tk), :]
        s_buf[j & 1] = lax.dot_general(kj, qd_buf[par], (((1,), (1,)), ((), ())),
                                       preferred_element_type=F32)

    def softmax(par, j, diag):
        s = s_buf[j & 1]
        if diag is not None:
            ki = lax.broadcasted_iota(jnp.int32, s.shape, 0) + diag * tk
            qi = lax.broadcasted_iota(jnp.int32, s.shape, 1) & (tq - 1)
            s = jnp.where(ki <= qi, s, -jnp.inf)
        m_old = m_ref[par]
        m_new = jnp.maximum(m_old, jnp.max(s, axis=0, keepdims=True))
        m_ref[par] = m_new
        al_buf[j & 1] = jnp.exp(m_old - m_new)
        p_buf[j & 1] = jnp.exp(s - m_new).astype(BF16)

    def values(par, j):
        vj = jnp.concatenate([vt_ref[j], ones], axis=0)
        acc_ref[par] = al_buf[j & 1] * acc_ref[par] + jnp.dot(vj, p_buf[j & 1], preferred_element_type=F32)

    for i in range(seq // tq):
        par = i & 1
        nb = per_q * (i + 1)
        diag_of = lambda j, nb=nb: j - (nb - per_q) if j >= nb - per_q else None
        q = q_ref[i * tq:(i + 1) * tq, :]
        zero = jnp.zeros_like(q)
        qd_buf[par] = jnp.concatenate([jnp.where(comp0, q, zero), jnp.where(comp0, zero, q)], axis=0)
        m_ref[par] = jnp.full(m_ref.shape[1:], -1e30, F32)
        acc_ref[par] = jnp.zeros(acc_ref.shape[1:], F32)

        scores(par, 0)
        softmax(par, 0, diag_of(0))
        scores(par, 1)
        n_plain = nb - per_q - 1
        if n_plain > 0:
            def steady(t, _, par=par):
                values(par, t)
                softmax(par, t + 1, None)
                scores(par, t + 2)
                return 0
            lax.fori_loop(0, n_plain, steady, 0)
        for t in range(max(n_plain, 0), nb - 2):
            values(par, t)
            softmax(par, t + 1, diag_of(t + 1))
            scores(par, t + 2)
        values(par, nb - 2)
        softmax(par, nb - 1, diag_of(nb - 1))
        values(par, nb - 1)

        acc = acc_ref[par]
        o = acc[:LANES] / acc[LANES:LANES + 1]
        out = (o[:, :tq] - lam * o[:, tq:]).T
        out = _rms(out, sn_ref[...], 1e-5) * (1.0 - lam_init)
        o_ref[i * tq:(i + 1) * tq, :] = out.astype(BF16)


def _attn(q3, k3, vt5, lam_vecs, sub_norm, lam_init, tq):
    bsz, seq, d = q3.shape
    heads = d // LANES
    nkv, _, tk = vt5.shape[2:]
    assert tq % tk == 0 and tq // tk >= 2 and seq % tq == 0
    head_rows = pl.BlockSpec((None, seq, LANES), lambda b, h: (b, 0, h))
    return pl.pallas_call(
        functools.partial(_attn_body, lam_init),
        out_shape=jax.ShapeDtypeStruct((bsz, seq, d), BF16),
        grid=(bsz, heads),
        in_specs=[head_rows, head_rows,
                  pl.BlockSpec((None, None, nkv, LANES, tk), lambda b, h: (b, h, 0, 0, 0)),
                  _const_spec((4, HEAD_DIM)), _const_spec((1, LANES))],
        out_specs=head_rows,
        scratch_shapes=[pltpu.VMEM((2, 2 * tq, LANES), BF16), pltpu.VMEM((2, tk, 2 * tq), F32),
                        pltpu.VMEM((2, tk, 2 * tq), BF16), pltpu.VMEM((2, 1, 2 * tq), F32),
                        pltpu.VMEM((2, 1, 2 * tq), F32), pltpu.VMEM((2, LANES + SUM_ROWS, 2 * tq), F32)],
        compiler_params=_params("parallel", "parallel"),
        name="diff_attn",
    )(q3, k3, vt5, lam_vecs, sub_norm)


def _proj_res_body(o_ref, x_ref, w_ref, out_ref):
    out_ref[...] = x_ref[...] + jnp.dot(o_ref[...], w_ref[...], preferred_element_type=F32)


def _proj_res(o2, x2, w, tm):
    t, d = x2.shape
    row = lambda i: (i, 0)
    return pl.pallas_call(
        _proj_res_body,
        out_shape=jax.ShapeDtypeStruct((t, d), F32),
        grid=(t // tm,),
        in_specs=[pl.BlockSpec((tm, d), row), pl.BlockSpec((tm, d), row), _const_spec((d, d))],
        out_specs=pl.BlockSpec((tm, d), row),
        compiler_params=_params("parallel"),
        name="proj_res",
    )(o2, x2, w)


def _block_ones(n, blk):
    idx = jnp.arange(n) // blk
    return (idx[:, None] == idx[None, :]).astype(BF16)


def _rope_tables(seq):
    inv = 1.0 / (ROPE_THETA ** (jnp.arange(0, HEAD_DIM, 2, dtype=F32) / HEAD_DIM))
    ang = jnp.arange(seq, dtype=F32)[:, None] * inv[None, :]
    cos, sin = jnp.cos(ang), jnp.sin(ang)
    reps = LANES // HEAD_DIM
    return jnp.tile(jnp.concatenate([cos, cos], -1), (1, reps)), jnp.tile(jnp.concatenate([-sin, sin], -1), (1, reps))


def _row_tile(seq):
    return min(256, seq)


def kernel(x, ffn_norm, ffn_w_gate, ffn_w_up, ffn_w_down, mix_norm, a_w_in, a_mu, a_w0, a_w2, a_a0, a_a2, a_g2, a_k_k, a_k_a, a_r_k, a_ln_w, a_ln_b, b_glu_bias, b_dw, b_dw_bias, b_ln_w, b_ln_b, e_w_out, c_w_in, c_q_norm, c_k_norm, c_lq1, c_lk1, c_lq2, c_lk2, c_sub_norm, c_w_out):
    bsz, seq, d = x.shape
    depth = ffn_norm.shape[0]
    t = bsz * seq
    tm = _row_tile(seq)
    x2 = x.reshape(t, d)
    wg, wu, wd = ffn_w_gate.astype(BF16), ffn_w_up.astype(BF16), ffn_w_down.astype(BF16)
    cos, sin = _rope_tables(seq)

    def ffn(x2, layer, which):
        return _ffn(x2, ffn_norm[layer, which], wg[layer, which], wu[layer, which], wd[layer, which], tm)

    for layer in range(depth):
        x2 = ffn(x2, layer, 0)
        j = layer // 2
        if layer % 2 == 0:
            aw = a_w0.shape[1]
            a_proj = a_mu.shape[1]
            row = lambda v: v.reshape(1, -1)
            sel = _block_ones(aw, HEAD_DIM)
            za, zb = _even_in(x2, mix_norm[layer], a_w_in[j].astype(BF16), a_proj, tm)
            w2p = jnp.pad(a_w2[j], ((0, ICL_RANK), (0, 0)))
            w2h = w2p.astype(BF16)
            w2l = (w2p - w2h.astype(F32)).astype(BF16)
            a2p = jnp.pad(a_a2[j], ((DECAY_RANK, 0), (0, 0))).astype(BF16)
            r, lw, k, v, kk, a, g, bonus = _rwkv_prep(
                za, bsz, seq, aw, row(a_mu[j]), row(a_w0[j]), w2h, w2l, row(a_a0[j]), a2p, a_g2[j].astype(BF16),
                row(a_k_k[j]), row(a_k_a[j]), row(a_r_k[j]), sel, tm)
            yb = _conv(zb, bsz, seq, row(b_glu_bias[j]), jnp.pad(b_dw[j], ((0, CONV_HALO - CONV_WIDTH), (0, 0))),
                       row(b_dw_bias[j]), row(b_ln_w[j]), row(b_ln_b[j]), tm)
            y = _wkv(r, lw, k, v, kk, a, bsz, seq)
            x2 = _even_out(y, bonus, g, yb, x2, row(a_ln_w[j]), row(a_ln_b[j]), sel, e_w_out[j].astype(BF16), tm)
        else:
            lam_init = 0.8 - 0.6 * math.exp(-0.3 * layer)
            reps = d // HEAD_DIM
            q, k, v = _odd_in(x2, bsz, seq, mix_norm[layer], c_w_in[j].astype(BF16),
                              jnp.tile(c_q_norm[j], reps).reshape(1, d), jnp.tile(c_k_norm[j], reps).reshape(1, d),
                              cos, sin, _block_ones(LANES, HEAD_DIM), tm)
            lam_vecs = jnp.stack([c_lq1[j], c_lk1[j], c_lq2[j], c_lk2[j]])
            vt5 = v.reshape(bsz, seq // ATTN_TK, ATTN_TK, d // LANES, LANES).transpose(0, 3, 1, 4, 2)
            o = _attn(q.reshape(bsz, seq, d), k.reshape(bsz, seq, d), vt5, lam_vecs,
                      c_sub_norm[j].reshape(1, LANES), lam_init, ATTN_TQ)
            x2 = _proj_res(o.reshape(t, d), x2, c_w_out[j].astype(BF16), tm)
        x2 = ffn(x2, layer, 1)
    return x2.reshape(bsz, seq, d)
```

```python
import functools
import math

import jax
import jax.numpy as jnp
from jax import lax
from jax.experimental import pallas as pl
from jax.experimental.pallas import tpu as pltpu

F32 = jnp.float32
BF16 = jnp.bfloat16

HEAD_DIM = 64
LANES = 128
CONV_WIDTH = 31
CONV_HALO = 32
DECAY_RANK = 64
ICL_RANK = 64
GATE_RANK = 128
GN_EPS = 64e-5
ROPE_THETA = 10000.0
ATTN_SCALE = HEAD_DIM ** -0.5
CHUNK = 64
VMEM_LIMIT = 56 * 1024 * 1024


def _const_spec(shape):
    return pl.BlockSpec(shape, lambda *_: (0,) * len(shape), pipeline_mode=pl.Buffered(1))


def _params(*sem):
    return pltpu.CompilerParams(dimension_semantics=sem, vmem_limit_bytes=VMEM_LIMIT)


def _dot(a, b):
    return jnp.dot(a.astype(BF16), b.astype(BF16), preferred_element_type=F32)


def _dot_nt(a, b):
    return lax.dot_general(a.astype(BF16), b.astype(BF16), (((1,), (1,)), ((), ())),
                           preferred_element_type=F32)


def _dot_tn(a, b):
    return lax.dot_general(a.astype(BF16), b.astype(BF16), (((0,), (0,)), ((), ())),
                           preferred_element_type=F32)


def _split2(x):
    hi = x.astype(BF16)
    lo = (x - hi.astype(F32)).astype(BF16)
    return hi, lo


def _split3(x):
    hi = x.astype(BF16)
    r1 = x - hi.astype(F32)
    mid = r1.astype(BF16)
    lo = (r1 - mid.astype(F32)).astype(BF16)
    return hi, mid, lo


def _dot_sel(x, sel):
    hi, lo = _split2(x)
    return (jnp.dot(hi, sel, preferred_element_type=F32)
            + jnp.dot(lo, sel, preferred_element_type=F32))


def _dot_x3(a, b_hi, b_lo):
    a_hi, a_lo = _split2(a)
    return (jnp.dot(a_hi, b_hi, preferred_element_type=F32)
            + jnp.dot(a_lo, b_hi, preferred_element_type=F32)
            + jnp.dot(a_hi, b_lo, preferred_element_type=F32))


def _rms(x, g, eps):
    return x * lax.rsqrt(jnp.mean(x * x, axis=-1, keepdims=True) + eps) * g


def _sigmoid(x):
    return 1.0 / (1.0 + jnp.exp(-x))


def _ffn_body(x_ref, g_ref, wg_ref, wu_ref, wd_ref, o_ref):
    x = x_ref[...]
    h = _rms(x, g_ref[...], 1e-6).astype(BF16)
    gate = jnp.dot(h, wg_ref[...], preferred_element_type=F32)
    up = jnp.dot(h, wu_ref[...], preferred_element_type=F32)
    act = (gate * _sigmoid(gate) * up).astype(BF16)
    o_ref[...] = x + 0.5 * jnp.dot(act, wd_ref[...], preferred_element_type=F32)


def _ffn(x2, g, wg, wu, wd, tm):
    t, d = x2.shape
    f = wg.shape[1]
    return pl.pallas_call(
        _ffn_body,
        out_shape=jax.ShapeDtypeStruct((t, d), F32),
        grid=(t // tm,),
        in_specs=[pl.BlockSpec((tm, d), lambda i: (i, 0)), _const_spec((1, d)),
                  _const_spec((d, f)), _const_spec((d, f)), _const_spec((f, d))],
        out_specs=pl.BlockSpec((tm, d), lambda i: (i, 0)),
        compiler_params=_params("parallel"),
        name="ffn",
    )(x2, g.reshape(1, d), wg, wu, wd)


def _even_in_body(a_proj, x_ref, g_ref, w_ref, za_ref, zb_ref):
    h = _rms(x_ref[...], g_ref[...], 1e-6).astype(BF16)
    z = jnp.dot(h, w_ref[...], preferred_element_type=F32)
    za_ref[...] = z[:, :a_proj]
    zb_ref[...] = z[:, a_proj:]


def _even_in(x2, g, w, a_proj, tm):
    t, d = x2.shape
    n = w.shape[1]
    return pl.pallas_call(
        functools.partial(_even_in_body, a_proj),
        out_shape=(jax.ShapeDtypeStruct((t, a_proj), F32), jax.ShapeDtypeStruct((t, n - a_proj), F32)),
        grid=(t // tm,),
        in_specs=[pl.BlockSpec((tm, d), lambda i: (i, 0)), _const_spec((1, d)), _const_spec((d, n))],
        out_specs=(pl.BlockSpec((tm, a_proj), lambda i: (i, 0)),
                   pl.BlockSpec((tm, n - a_proj), lambda i: (i, 0))),
        compiler_params=_params("parallel"),
        name="even_in",
    )(x2, g.reshape(1, d), w)


def _rwkv_prep_body(aw, za_ref, mu_ref, w0_ref, w2h_ref, w2l_ref, a0_ref, a2_ref, g2_ref, kk_ref_w, ka_ref,
                    rk_ref, sel_ref, r_out, lw_out, k_out, v_out, kk_out, a_out, g_out, bonus_out, prev_ref):
    tm = za_ref.shape[0]
    j = pl.program_id(1)

    @pl.when(j == 0)
    def _():
        prev_ref[...] = jnp.zeros_like(prev_ref)

    za = za_ref[...]
    row = lax.broadcasted_iota(jnp.int32, za.shape, 0)
    shifted = jnp.where(row == 0, prev_ref[0:1, :], pltpu.roll(za, 1, 0))
    prev_ref[0:1, :] = za[tm - 1:tm, :]
    xr = za + (shifted - za) * mu_ref[...]

    r = xr[:, 0:aw]
    k = xr[:, aw:2 * aw]
    v = xr[:, 2 * aw:3 * aw]
    lr = xr[:, 3 * aw:3 * aw + DECAY_RANK + ICL_RANK]
    gd = xr[:, 3 * aw + DECAY_RANK + ICL_RANK:]

    pre = w0_ref[...] + _dot_x3(jnp.tanh(lr), w2h_ref[...], w2l_ref[...])
    softplus_neg = jnp.maximum(-pre, 0.0) + jnp.log(1.0 + jnp.exp(-jnp.abs(pre)))
    w_log = -softplus_neg - 0.5
    lw_out[...] = -jnp.exp(w_log)

    a = _sigmoid(a0_ref[...] + jnp.dot(lr.astype(BF16), a2_ref[...], preferred_element_type=F32))
    g_out[...] = jnp.dot(_sigmoid(gd).astype(BF16), g2_ref[...], preferred_element_type=F32)

    sel = sel_ref[...]
    kk = k * kk_ref_w[...]
    kk = kk / jnp.maximum(jnp.sqrt(_dot_sel(kk * kk, sel)), 1e-12)
    kmod = k * (1.0 + (a - 1.0) * ka_ref[...])
    bonus_out[...] = _dot_sel(r * kmod * rk_ref[...], sel) * v
    r_out[...] = r
    k_out[...] = kmod
    v_out[...] = v
    kk_out[...] = kk
    a_out[...] = a


def _rwkv_prep(za, bsz, seq, aw, mu, w0, w2h, w2l, a0, a2p, g2, k_k, k_a, r_k, sel, tm):
    t, ap = za.shape
    nt = seq // tm
    row = lambda b, j: (b * nt + j, 0)
    vec = lambda n: _const_spec((1, n))
    out = jax.ShapeDtypeStruct((t, aw), F32)
    lora = DECAY_RANK + ICL_RANK
    return pl.pallas_call(
        functools.partial(_rwkv_prep_body, aw),
        out_shape=(out,) * 8,
        grid=(bsz, nt),
        in_specs=[pl.BlockSpec((tm, ap), row), vec(ap), vec(aw), _const_spec((lora, aw)), _const_spec((lora, aw)),
                  vec(aw), _const_spec((lora, aw)), _const_spec((GATE_RANK, aw)), vec(aw), vec(aw), vec(aw),
                  _const_spec((aw, aw))],
        out_specs=(pl.BlockSpec((tm, aw), row),) * 8,
        scratch_shapes=[pltpu.VMEM((8, ap), F32)],
        compiler_params=_params("parallel", "arbitrary"),
        name="rwkv_prep",
    )(za, mu, w0, w2h, w2l, a0, a2p, g2, k_k, k_a, r_k, sel)


def _conv_body(bw, zb_ref, bias_ref, dw_ref, dwb_ref, lnw_ref, lnb_ref, y_ref, buf_ref):
    tm = zb_ref.shape[0]
    j = pl.program_id(1)
    u = zb_ref[...] + bias_ref[...]
    gl = u[:, :bw] * _sigmoid(u[:, bw:])

    @pl.when(j == 0)
    def _():
        buf_ref[0:CONV_HALO, :] = jnp.zeros((CONV_HALO, bw), F32)

    @pl.when(j > 0)
    def _():
        buf_ref[0:CONV_HALO, :] = buf_ref[tm:tm + CONV_HALO, :]

    buf_ref[CONV_HALO:, :] = gl
    base = CONV_HALO - (CONV_WIDTH - 1)
    acc = jnp.zeros((tm, bw), F32)
    for tap in range(CONV_WIDTH):
        acc = acc + dw_ref[tap:tap + 1, :] * buf_ref[pl.ds(base + tap, tm), :]
    c = acc + dwb_ref[...]
    mu = jnp.mean(c, axis=-1, keepdims=True)
    var = jnp.mean(jnp.square(c - mu), axis=-1, keepdims=True)
    y = (c - mu) * lax.rsqrt(var + 1e-5) * lnw_ref[...] + lnb_ref[...]
    y_ref[...] = y * _sigmoid(y)


def _conv(zb, bsz, seq, bias, dw, dwb, lnw, lnb, tm):
    t, two_bw = zb.shape
    bw = two_bw // 2
    nt = seq // tm
    row = lambda b, j: (b * nt + j, 0)
    return pl.pallas_call(
        functools.partial(_conv_body, bw),
        out_shape=jax.ShapeDtypeStruct((t, bw), F32),
        grid=(bsz, nt),
        in_specs=[pl.BlockSpec((tm, two_bw), row), _const_spec((1, two_bw)), _const_spec((CONV_HALO, bw)),
                  _const_spec((1, bw)), _const_spec((1, bw)), _const_spec((1, bw))],
        out_specs=pl.BlockSpec((tm, bw), row),
        scratch_shapes=[pltpu.VMEM((tm + CONV_HALO, bw), F32)],
        compiler_params=_params("parallel", "arbitrary"),
        name="conv_module",
    )(zb, bias, dw, dwb, lnw, lnb)


def _wkv_body(r_ref, lw_ref, k_ref, v_ref, kk_ref, a_ref, y_ref, state_ref):
    nseq, ch, width = r_ref.shape
    n_pairs = width // LANES
    c = pl.program_id(1)

    @pl.when(c == 0)
    def _():
        state_ref[...] = jnp.zeros_like(state_ref)

    ti = lax.broadcasted_iota(jnp.int32, (ch, ch), 0)
    tj = lax.broadcasted_iota(jnp.int32, (ch, ch), 1)
    tri = (ti >= tj).astype(BF16)
    lane = lax.broadcasted_iota(jnp.int32, (ch, LANES), 1)
    head0 = lane < HEAD_DIM

    def stack(x):
        zero = jnp.zeros_like(x)
        return jnp.concatenate([jnp.where(head0, x, zero), jnp.where(head0, zero, x)], axis=0)

    n = 2 * ch
    ri = lax.broadcasted_iota(jnp.int32, (n, n), 0) & (ch - 1)
    ci = lax.broadcasted_iota(jnp.int32, (n, n), 1) & (ch - 1)
    strict = ri > ci
    incl = ri >= ci
    eye = (lax.broadcasted_iota(jnp.int32, (n, n), 0) == lax.broadcasted_iota(jnp.int32, (n, n), 1)).astype(F32)

    rd, ad, bd, kd, bhd, khd, vd, g_last, where = [], [], [], [], [], [], [], [], []
    for s in range(nseq):
        lw = lw_ref[s]
        lg = sum(jnp.dot(tri, part, preferred_element_type=F32) for part in _split3(lw))
        lg_last = lg[ch - 1:ch, :]
        e_pos = jnp.exp(lg)
        e_neg = jnp.exp(-lg)
        e_prev = jnp.exp(lg - lw)
        e_tail = jnp.exp(lg_last - lg)
        kk = kk_ref[s]
        kka = kk * a_ref[s]
        kmod = k_ref[s]
        r_t = (r_ref[s] * e_pos).astype(BF16)
        a_t = (-kk * e_prev).astype(BF16)
        b_t = (kka * e_neg).astype(BF16)
        k_t = (kmod * e_neg).astype(BF16)
        b_h = (kka * e_tail).astype(BF16)
        k_h = (kmod * e_tail).astype(BF16)
        v_b = v_ref[s].astype(BF16)
        g_s = jnp.exp(lg_last)
        for p in range(n_pairs):
            sl = slice(p * LANES, (p + 1) * LANES)
            rd.append(stack(r_t[:, sl]))
            ad.append(stack(a_t[:, sl]))
            bd.append(stack(b_t[:, sl]))
            kd.append(stack(k_t[:, sl]))
            bhd.append(stack(b_h[:, sl]))
            khd.append(stack(k_h[:, sl]))
            vd.append(stack(v_b[:, sl]))
            g_last.append(g_s[:, sl])
            where.append((s, p, sl))
    units = range(len(where))

    ob = [_dot_nt(jnp.concatenate([ad[u], rd[u]], axis=0), bd[u]) for u in units]
    ok = [_dot_nt(jnp.concatenate([ad[u], rd[u]], axis=0), kd[u]) for u in units]
    a_ab = [jnp.where(strict, ob[u][:n], 0.0) for u in units]
    a_rb = [jnp.where(incl, ob[u][n:], 0.0).astype(BF16) for u in units]
    a_ak = [jnp.where(strict, ok[u][:n], 0.0) for u in units]
    a_rk = [jnp.where(incl, ok[u][n:], 0.0).astype(BF16) for u in units]
    akv = [_dot(a_ak[u], vd[u]) for u in units]

    tinv = [eye + a_ab[u] for u in units]
    pw = [_dot(a_ab[u], a_ab[u]) for u in units]
    for _ in range(int(math.log2(ch)) - 2):
        both = [_dot(jnp.concatenate([tinv[u], pw[u]], axis=0), pw[u]) for u in units]
        tinv = [tinv[u] + both[u][:n] for u in units]
        pw = [both[u][n:] for u in units]
    tinv = [tinv[u] + _dot(tinv[u], pw[u]) for u in units]

    w12 = [_dot(tinv[u], jnp.concatenate([ad[u], akv[u].astype(BF16)], axis=1)) for u in units]
    s_old = [state_ref[s, p] for s, p, _ in where]
    from_state = [_dot_nt(jnp.concatenate([w12[u][:, :LANES].astype(BF16), rd[u]], axis=0), s_old[u]) for u in units]
    uv = [jnp.concatenate([(from_state[u][:n] + w12[u][:, LANES:]).astype(BF16), vd[u]], axis=0) for u in units]
    y = [from_state[u][n:] + _dot(jnp.concatenate([a_rb[u], a_rk[u]], axis=1), uv[u]) for u in units]
    for u, (s, p, sl) in enumerate(where):
        y_ref[s, :, sl] = y[u][:ch] + y[u][ch:]
        state_ref[s, p] = s_old[u] * g_last[u] + _dot_tn(uv[u], jnp.concatenate([bhd[u], khd[u]], axis=0))


WKV_SEQS = 2


def _wkv(r, lw, k, v, kk, a, bsz, seq):
    width = r.shape[-1]
    nc = seq // CHUNK
    nseq = WKV_SEQS if bsz % WKV_SEQS == 0 else 1
    spec = pl.BlockSpec((nseq, CHUNK, width), lambda b, c: (b, c, 0))
    args = [t.reshape(bsz, seq, width) for t in (r, lw, k, v, kk, a)]
    y = pl.pallas_call(
        _wkv_body,
        out_shape=jax.ShapeDtypeStruct((bsz, seq, width), F32),
        grid=(bsz // nseq, nc),
        in_specs=[spec] * 6,
        out_specs=spec,
        scratch_shapes=[pltpu.VMEM((nseq, width // LANES, LANES, LANES), F32)],
        compiler_params=_params("parallel", "arbitrary"),
        name="wkv7_chunk",
    )(*args)
    return y.reshape(bsz * seq, width)


def _even_out_body(aw, y_ref, bonus_ref, g_ref, yb_ref, x_ref, lnw_ref, lnb_ref, sel_ref, w_ref, o_ref):
    sel = sel_ref[...]
    y = y_ref[...]
    inv_n = 1.0 / HEAD_DIM
    d = y - _dot_sel(y, sel) * inv_n
    var = _dot_sel(d * d, sel) * inv_n
    yn = d * lax.rsqrt(var + GN_EPS) * lnw_ref[...] + lnb_ref[...]
    ya = ((yn + bonus_ref[...]) * g_ref[...]).astype(BF16)
    o_ref[...] = (x_ref[...]
                  + jnp.dot(ya, w_ref[0:aw, :], preferred_element_type=F32)
                  + jnp.dot(yb_ref[...].astype(BF16), w_ref[aw:, :], preferred_element_type=F32))


def _even_out(y, bonus, g, yb, x2, lnw, lnb, sel, w, tm):
    t, d = x2.shape
    aw = y.shape[1]
    bw = yb.shape[1]
    row = lambda i: (i, 0)
    return pl.pallas_call(
        functools.partial(_even_out_body, aw),
        out_shape=jax.ShapeDtypeStruct((t, d), F32),
        grid=(t // tm,),
        in_specs=[pl.BlockSpec((tm, aw), row), pl.BlockSpec((tm, aw), row), pl.BlockSpec((tm, aw), row),
                  pl.BlockSpec((tm, bw), row), pl.BlockSpec((tm, d), row), _const_spec((1, aw)),
                  _const_spec((1, aw)), _const_spec((aw, aw)), _const_spec((d, d))],
        out_specs=pl.BlockSpec((tm, d), row),
        compiler_params=_params("parallel"),
        name="even_out",
    )(y, bonus, g, yb, x2, lnw, lnb, sel, w)


def _odd_in_body(d, x_ref, g_ref, w_ref, qn_ref, kn_ref, cos_ref, sin_ref, sel_ref, q_ref, k_ref, v_ref):
    h = _rms(x_ref[...], g_ref[...], 1e-6).astype(BF16)
    z = jnp.dot(h, w_ref[...], preferred_element_type=F32)
    tm = z.shape[0]
    sel = sel_ref[...]
    cos = cos_ref[...]
    sin = sin_ref[...]
    lane = lax.broadcasted_iota(jnp.int32, (tm, LANES), 1)
    first_half = (lane & (HEAD_DIM - 1)) < HEAD_DIM // 2
    for blk in range(d // LANES):
        sl = slice(blk * LANES, (blk + 1) * LANES)
        for off, norm_ref, out_ref, scale in ((0, qn_ref, q_ref, ATTN_SCALE), (d, kn_ref, k_ref, 1.0)):
            xq = z[:, off + blk * LANES:off + (blk + 1) * LANES]
            ms = _dot_sel(xq * xq, sel) * (1.0 / HEAD_DIM)
            xn = xq * lax.rsqrt(ms + 1e-6) * norm_ref[:, sl]
            rot = jnp.where(first_half, pltpu.roll(xn, LANES - HEAD_DIM // 2, 1), pltpu.roll(xn, HEAD_DIM // 2, 1))
            out_ref[:, sl] = ((xn * cos + rot * sin) * scale).astype(BF16)
    v_ref[...] = z[:, 2 * d:].astype(BF16)


def _odd_in(x2, bsz, seq, g, w, qn, kn, cos, sin, sel, tm):
    t, d = x2.shape
    nt = seq // tm
    row = lambda b, j: (b * nt + j, 0)
    pos = lambda b, j: (j, 0)
    out = jax.ShapeDtypeStruct((t, d), BF16)
    return pl.pallas_call(
        functools.partial(_odd_in_body, d),
        out_shape=(out, out, out),
        grid=(bsz, nt),
        in_specs=[pl.BlockSpec((tm, d), row), _const_spec((1, d)), _const_spec((d, 3 * d)), _const_spec((1, d)),
                  _const_spec((1, d)), pl.BlockSpec((tm, LANES), pos), pl.BlockSpec((tm, LANES), pos),
                  _const_spec((LANES, LANES))],
        out_specs=(pl.BlockSpec((tm, d), row),) * 3,
        compiler_params=_params("parallel", "parallel"),
        name="odd_in",
    )(x2, g.reshape(1, d), w, qn, kn, cos, sin, sel)


ATTN_TQ = 512
ATTN_TK = 256
SUM_ROWS = 16


def _attn_body(lam_init, q_ref, k_ref, vt_ref, lam_ref, sn_ref, o_ref, qd_buf, s_buf, p_buf, al_buf, m_ref, acc_ref):
    seq = q_ref.shape[0]
    tk = vt_ref.shape[-1]
    tq = qd_buf.shape[1] // 2
    per_q = tq // tk
    ones = jnp.ones((SUM_ROWS, tk), BF16)
    lv = lam_ref[...]
    lam = (jnp.exp(jnp.sum(lv[0:1] * lv[1:2], axis=-1, keepdims=True))
           - jnp.exp(jnp.sum(lv[2:3] * lv[3:4], axis=-1, keepdims=True)) + lam_init)
    lane = lax.broadcasted_iota(jnp.int32, (tq, LANES), 1)
    comp0 = lane < HEAD_DIM

    def scores(par, j):
        start = j * tk if isinstance(j, int) else pl.multiple_of(j * tk, tk)
        kj = k_ref[pl.ds(start, tk), :]
        s_buf[j & 1] = lax.dot_general(kj, qd_buf[par], (((1,), (1,)), ((), ())),
                                       preferred_element_type=F32)

    def softmax(par, j, diag):
        s = s_buf[j & 1]
        if diag is not None:
            ki = lax.broadcasted_iota(jnp.int32, s.shape, 0) + diag * tk
            qi = lax.broadcasted_iota(jnp.int32, s.shape, 1) & (tq - 1)
            s = jnp.where(ki <= qi, s, -jnp.inf)
        m_old = m_ref[par]
        m_new = jnp.maximum(m_old, jnp.max(s, axis=0, keepdims=True))
        m_ref[par] = m_new
        al_buf[j & 1] = jnp.exp(m_old - m_new)
        p_buf[j & 1] = jnp.exp(s - m_new).astype(BF16)

    def values(par, j):
        vj = jnp.concatenate([vt_ref[j], ones], axis=0)
        acc_ref[par] = al_buf[j & 1] * acc_ref[par] + jnp.dot(vj, p_buf[j & 1], preferred_element_type=F32)

    for i in range(seq // tq):
        par = i & 1
        nb = per_q * (i + 1)
        diag_of = lambda j, nb=nb: j - (nb - per_q) if j >= nb - per_q else None
        q = q_ref[i * tq:(i + 1) * tq, :]
        zero = jnp.zeros_like(q)
        qd_buf[par] = jnp.concatenate([jnp.where(comp0, q, zero), jnp.where(comp0, zero, q)], axis=0)
        m_ref[par] = jnp.full(m_ref.shape[1:], -1e30, F32)
        acc_ref[par] = jnp.zeros(acc_ref.shape[1:], F32)

        scores(par, 0)
        softmax(par, 0, diag_of(0))
        scores(par, 1)
        n_plain = nb - per_q - 1
        if n_plain > 0:
            def steady(t, _, par=par):
                values(par, t)
                softmax(par, t + 1, None)
                scores(par, t + 2)
                return 0
            lax.fori_loop(0, n_plain, steady, 0)
        for t in range(max(n_plain, 0), nb - 2):
            values(par, t)
            softmax(par, t + 1, diag_of(t + 1))
            scores(par, t + 2)
        values(par, nb - 2)
        softmax(par, nb - 1, diag_of(nb - 1))
        values(par, nb - 1)

        acc = acc_ref[par]
        o = acc[:LANES] / acc[LANES:LANES + 1]
        out = (o[:, :tq] - lam * o[:, tq:]).T
        out = _rms(out, sn_ref[...], 1e-5) * (1.0 - lam_init)
        o_ref[i * tq:(i + 1) * tq, :] = out.astype(BF16)


def _attn(q3, k3, vt5, lam_vecs, sub_norm, lam_init, tq):
    bsz, seq, d = q3.shape
    heads = d // LANES
    nkv, _, tk = vt5.shape[2:]
    assert tq % tk == 0 and tq // tk >= 2 and seq % tq == 0
    head_rows = pl.BlockSpec((None, seq, LANES), lambda b, h: (b, 0, h))
    return pl.pallas_call(
        functools.partial(_attn_body, lam_init),
        out_shape=jax.ShapeDtypeStruct((bsz, seq, d), BF16),
        grid=(bsz, heads),
        in_specs=[head_rows, head_rows,
                  pl.BlockSpec((None, None, nkv, LANES, tk), lambda b, h: (b, h, 0, 0, 0)),
                  _const_spec((4, HEAD_DIM)), _const_spec((1, LANES))],
        out_specs=head_rows,
        scratch_shapes=[pltpu.VMEM((2, 2 * tq, LANES), BF16), pltpu.VMEM((2, tk, 2 * tq), F32),
                        pltpu.VMEM((2, tk, 2 * tq), BF16), pltpu.VMEM((2, 1, 2 * tq), F32),
                        pltpu.VMEM((2, 1, 2 * tq), F32), pltpu.VMEM((2, LANES + SUM_ROWS, 2 * tq), F32)],
        compiler_params=_params("parallel", "parallel"),
        name="diff_attn",
    )(q3, k3, vt5, lam_vecs, sub_norm)


def _proj_res_body(o_ref, x_ref, w_ref, out_ref):
    out_ref[...] = x_ref[...] + jnp.dot(o_ref[...], w_ref[...], preferred_element_type=F32)


def _proj_res(o2, x2, w, tm):
    t, d = x2.shape
    row = lambda i: (i, 0)
    return pl.pallas_call(
        _proj_res_body,
        out_shape=jax.ShapeDtypeStruct((t, d), F32),
        grid=(t // tm,),
        in_specs=[pl.BlockSpec((tm, d), row), pl.BlockSpec((tm, d), row), _const_spec((d, d))],
        out_specs=pl.BlockSpec((tm, d), row),
        compiler_params=_params("parallel"),
        name="proj_res",
    )(o2, x2, w)


def _block_ones(n, blk):
    idx = jnp.arange(n) // blk
    return (idx[:, None] == idx[None, :]).astype(BF16)


def _rope_tables(seq):
    inv = 1.0 / (ROPE_THETA ** (jnp.arange(0, HEAD_DIM, 2, dtype=F32) / HEAD_DIM))
    ang = jnp.arange(seq, dtype=F32)[:, None] * inv[None, :]
    cos, sin = jnp.cos(ang), jnp.sin(ang)
    reps = LANES // HEAD_DIM
    return jnp.tile(jnp.concatenate([cos, cos], -1), (1, reps)), jnp.tile(jnp.concatenate([-sin, sin], -1), (1, reps))


def _row_tile(seq):
    return min(256, seq)


def kernel(x, ffn_norm, ffn_w_gate, ffn_w_up, ffn_w_down, mix_norm, a_w_in, a_mu, a_w0, a_w2, a_a0, a_a2, a_g2, a_k_k, a_k_a, a_r_k, a_ln_w, a_ln_b, b_glu_bias, b_dw, b_dw_bias, b_ln_w, b_ln_b, e_w_out, c_w_in, c_q_norm, c_k_norm, c_lq1, c_lk1, c_lq2, c_lk2, c_sub_norm, c_w_out):
    bsz, seq, d = x.shape
    depth = ffn_norm.shape[0]
    t = bsz * seq
    tm = _row_tile(seq)
    x2 = x.reshape(t, d)
    wg, wu, wd = ffn_w_gate.astype(BF16), ffn_w_up.astype(BF16), ffn_w_down.astype(BF16)
    cos, sin = _rope_tables(seq)

    def ffn(x2, layer, which):
        return _ffn(x2, ffn_norm[layer, which], wg[layer, which], wu[layer, which], wd[layer, which], tm)

    for layer in range(depth):
        x2 = ffn(x2, layer, 0)
        j = layer // 2
        if layer % 2 == 0:
            aw = a_w0.shape[1]
            a_proj = a_mu.shape[1]
            row = lambda v: v.reshape(1, -1)
            sel = _block_ones(aw, HEAD_DIM)
            za, zb = _even_in(x2, mix_norm[layer], a_w_in[j].astype(BF16), a_proj, tm)
            w2p = jnp.pad(a_w2[j], ((0, ICL_RANK), (0, 0)))
            w2h = w2p.astype(BF16)
            w2l = (w2p - w2h.astype(F32)).astype(BF16)
            a2p = jnp.pad(a_a2[j], ((DECAY_RANK, 0), (0, 0))).astype(BF16)
            r, lw, k, v, kk, a, g, bonus = _rwkv_prep(
                za, bsz, seq, aw, row(a_mu[j]), row(a_w0[j]), w2h, w2l, row(a_a0[j]), a2p, a_g2[j].astype(BF16),
                row(a_k_k[j]), row(a_k_a[j]), row(a_r_k[j]), sel, tm)
            yb = _conv(zb, bsz, seq, row(b_glu_bias[j]), jnp.pad(b_dw[j], ((0, CONV_HALO - CONV_WIDTH), (0, 0))),
                       row(b_dw_bias[j]), row(b_ln_w[j]), row(b_ln_b[j]), tm)
            y = _wkv(r, lw, k, v, kk, a, bsz, seq)
            x2 = _even_out(y, bonus, g, yb, x2, row(a_ln_w[j]), row(a_ln_b[j]), sel, e_w_out[j].astype(BF16), tm)
        else:
            lam_init = 0.8 - 0.6 * math.exp(-0.3 * layer)
            reps = d // HEAD_DIM
            q, k, v = _odd_in(x2, bsz, seq, mix_norm[layer], c_w_in[j].astype(BF16),
                              jnp.tile(c_q_norm[j], reps).reshape(1, d), jnp.tile(c_k_norm[j], reps).reshape(1, d),
                              cos, sin, _block_ones(LANES, HEAD_DIM), tm)
            lam_vecs = jnp.stack([c_lq1[j], c_lk1[j], c_lq2[j], c_lk2[j]])
            vt5 = v.reshape(bsz, seq // ATTN_TK, ATTN_TK, d // LANES, LANES).transpose(0, 3, 1, 4, 2)
            o = _attn(q.reshape(bsz, seq, d), k.reshape(bsz, seq, d), vt5, lam_vecs,
                      c_sub_norm[j].reshape(1, LANES), lam_init, ATTN_TQ)
            x2 = _proj_res(o.reshape(t, d), x2, c_w_out[j].astype(BF16), tm)
        x2 = ffn(x2, layer, 1)
    return x2.reshape(bsz, seq, d)
```

```python
import functools
import math

import jax
import jax.numpy as jnp
from jax import lax
from jax.experimental import pallas as pl
from jax.experimental.pallas import tpu as pltpu

F32 = jnp.float32
BF16 = jnp.bfloat16

HEAD_DIM = 64
LANES = 128
CONV_WIDTH = 31
CONV_HALO = 32
DECAY_RANK = 64
ICL_RANK = 64
GATE_RANK = 128
GN_EPS = 64e-5
ROPE_THETA = 10000.0
ATTN_SCALE = HEAD_DIM ** -0.5
CHUNK = 64
VMEM_LIMIT = 56 * 1024 * 1024


def _const_spec(shape):
    return pl.BlockSpec(shape, lambda *_: (0,) * len(shape), pipeline_mode=pl.Buffered(1))


def _params(*sem):
    return pltpu.CompilerParams(dimension_semantics=sem, vmem_limit_bytes=VMEM_LIMIT)


def _dot(a, b):
    return jnp.dot(a.astype(BF16), b.astype(BF16), preferred_element_type=F32)


def _dot_nt(a, b):
    return lax.dot_general(a.astype(BF16), b.astype(BF16), (((1,), (1,)), ((), ())),
                           preferred_element_type=F32)


def _dot_tn(a, b):
    return lax.dot_general(a.astype(BF16), b.astype(BF16), (((0,), (0,)), ((), ())),
                           preferred_element_type=F32)


def _split2(x):
    hi = x.astype(BF16)
    lo = (x - hi.astype(F32)).astype(BF16)
    return hi, lo


def _split3(x):
    hi = x.astype(BF16)
    r1 = x - hi.astype(F32)
    mid = r1.astype(BF16)
    lo = (r1 - mid.astype(F32)).astype(BF16)
    return hi, mid, lo


def _dot_sel(x, sel):
    return jnp.dot(x.astype(BF16), sel, preferred_element_type=F32)


def _dot_x3(a, b_hi, b_lo):
    a_hi, a_lo = _split2(a)
    return (jnp.dot(a_hi, b_hi, preferred_element_type=F32)
            + jnp.dot(a_lo, b_hi, preferred_element_type=F32)
            + jnp.dot(a_hi, b_lo, preferred_element_type=F32))


def _rms(x, g, eps):
    return x * lax.rsqrt(jnp.mean(x * x, axis=-1, keepdims=True) + eps) * g


def _sigmoid(x):
    return 1.0 / (1.0 + jnp.exp(-x))


FFN_TM = 512


def _ffn_body(x_ref, g_ref, wg_ref, wu_ref, wd_ref, o_ref):
    x = x_ref[...]
    h = _rms(x, g_ref[...], 1e-6).astype(BF16)
    gate = jnp.dot(h, wg_ref[...], preferred_element_type=F32)
    up = jnp.dot(h, wu_ref[...], preferred_element_type=F32)
    act = (gate * _sigmoid(gate) * up).astype(BF16)
    o_ref[...] = x + 0.5 * jnp.dot(act, wd_ref[...], preferred_element_type=F32)


def _ffn(x2, g, wg, wu, wd, tm):
    t, d = x2.shape
    f = wg.shape[1]
    return pl.pallas_call(
        _ffn_body,
        out_shape=jax.ShapeDtypeStruct((t, d), F32),
        grid=(t // tm,),
        in_specs=[pl.BlockSpec((tm, d), lambda i: (i, 0)), _const_spec((1, d)),
                  _const_spec((d, f)), _const_spec((d, f)), _const_spec((f, d))],
        out_specs=pl.BlockSpec((tm, d), lambda i: (i, 0)),
        compiler_params=_params("parallel"),
        name="ffn",
    )(x2, g.reshape(1, d), wg, wu, wd)


def _even_in_body(a_proj, x_ref, g_ref, w_ref, za_ref, zb_ref):
    h = _rms(x_ref[...], g_ref[...], 1e-6).astype(BF16)
    z = jnp.dot(h, w_ref[...], preferred_element_type=F32)
    za_ref[...] = z[:, :a_proj]
    zb_ref[...] = z[:, a_proj:]


def _even_in(x2, g, w, a_proj, tm):
    t, d = x2.shape
    n = w.shape[1]
    return pl.pallas_call(
        functools.partial(_even_in_body, a_proj),
        out_shape=(jax.ShapeDtypeStruct((t, a_proj), F32), jax.ShapeDtypeStruct((t, n - a_proj), F32)),
        grid=(t // tm,),
        in_specs=[pl.BlockSpec((tm, d), lambda i: (i, 0)), _const_spec((1, d)), _const_spec((d, n))],
        out_specs=(pl.BlockSpec((tm, a_proj), lambda i: (i, 0)),
                   pl.BlockSpec((tm, n - a_proj), lambda i: (i, 0))),
        compiler_params=_params("parallel"),
        name="even_in",
    )(x2, g.reshape(1, d), w)


def _rwkv_prep_body(aw, za_ref, mu_ref, w0_ref, w2h_ref, w2l_ref, a0_ref, a2_ref, g2_ref, kk_ref_w, ka_ref,
                    rk_ref, sel_ref, r_out, lw_out, k_out, v_out, kk_out, a_out, g_out, bonus_out, prev_ref):
    tm = za_ref.shape[0]
    j = pl.program_id(1)

    @pl.when(j == 0)
    def _():
        prev_ref[...] = jnp.zeros_like(prev_ref)

    za = za_ref[...]
    row = lax.broadcasted_iota(jnp.int32, za.shape, 0)
    shifted = jnp.where(row == 0, prev_ref[0:1, :], pltpu.roll(za, 1, 0))
    prev_ref[0:1, :] = za[tm - 1:tm, :]
    xr = za + (shifted - za) * mu_ref[...]

    r = xr[:, 0:aw]
    k = xr[:, aw:2 * aw]
    v = xr[:, 2 * aw:3 * aw]
    lr = xr[:, 3 * aw:3 * aw + DECAY_RANK + ICL_RANK]
    gd = xr[:, 3 * aw + DECAY_RANK + ICL_RANK:]

    pre = w0_ref[...] + _dot_x3(jnp.tanh(lr), w2h_ref[...], w2l_ref[...])
    softplus_neg = jnp.maximum(-pre, 0.0) + jnp.log(1.0 + jnp.exp(-jnp.abs(pre)))
    w_log = -softplus_neg - 0.5
    lw_out[...] = -jnp.exp(w_log)

    a = _sigmoid(a0_ref[...] + jnp.dot(lr.astype(BF16), a2_ref[...], preferred_element_type=F32))
    g_out[...] = jnp.dot(_sigmoid(gd).astype(BF16), g2_ref[...], preferred_element_type=F32)

    sel = sel_ref[...]
    kk = k * kk_ref_w[...]
    kk = kk / jnp.maximum(jnp.sqrt(_dot_sel(kk * kk, sel)), 1e-12)
    kmod = k * (1.0 + (a - 1.0) * ka_ref[...])
    bonus_out[...] = _dot_sel(r * kmod * rk_ref[...], sel) * v
    r_out[...] = r
    k_out[...] = kmod
    v_out[...] = v
    kk_out[...] = kk
    a_out[...] = a


def _rwkv_prep(za, bsz, seq, aw, mu, w0, w2h, w2l, a0, a2p, g2, k_k, k_a, r_k, sel, tm):
    t, ap = za.shape
    nt = seq // tm
    row = lambda b, j: (b * nt + j, 0)
    vec = lambda n: _const_spec((1, n))
    out = jax.ShapeDtypeStruct((t, aw), F32)
    lora = DECAY_RANK + ICL_RANK
    return pl.pallas_call(
        functools.partial(_rwkv_prep_body, aw),
        out_shape=(out,) * 8,
        grid=(bsz, nt),
        in_specs=[pl.BlockSpec((tm, ap), row), vec(ap), vec(aw), _const_spec((lora, aw)), _const_spec((lora, aw)),
                  vec(aw), _const_spec((lora, aw)), _const_spec((GATE_RANK, aw)), vec(aw), vec(aw), vec(aw),
                  _const_spec((aw, aw))],
        out_specs=(pl.BlockSpec((tm, aw), row),) * 8,
        scratch_shapes=[pltpu.VMEM((8, ap), F32)],
        compiler_params=_params("parallel", "arbitrary"),
        name="rwkv_prep",
    )(za, mu, w0, w2h, w2l, a0, a2p, g2, k_k, k_a, r_k, sel)


def _conv_body(bw, zb_ref, bias_ref, dw_ref, dwb_ref, lnw_ref, lnb_ref, y_ref, buf_ref, shift_ref):
    tm = zb_ref.shape[0]
    j = pl.program_id(1)
    u = zb_ref[...] + bias_ref[...]
    gl = u[:, :bw] * _sigmoid(u[:, bw:])

    @pl.when(j == 0)
    def _():
        buf_ref[0:CONV_HALO, :] = jnp.zeros((CONV_HALO, bw), F32)

    @pl.when(j > 0)
    def _():
        buf_ref[0:CONV_HALO, :] = buf_ref[tm:tm + CONV_HALO, :]

    buf_ref[CONV_HALO:, :] = gl
    base = CONV_HALO - (CONV_WIDTH - 1)
    acc = jnp.zeros((tm, bw), F32)
    for shift in range(8):
        taps = [tap for tap in range(CONV_WIDTH) if (base + tap) % 8 == shift]
        if not taps:
            continue
        span = (base + taps[-1]) // 8 * 8
        if shift:
            shift_ref[shift - 1, 0:tm + span, :] = buf_ref[pl.ds(shift, tm + span), :]
        for tap in taps:
            off = (base + tap) // 8 * 8
            rows = shift_ref[shift - 1, off:off + tm, :] if shift else buf_ref[off:off + tm, :]
            acc = acc + dw_ref[tap:tap + 1, :] * rows
    c = acc + dwb_ref[...]
    mu = jnp.mean(c, axis=-1, keepdims=True)
    var = jnp.mean(jnp.square(c - mu), axis=-1, keepdims=True)
    y = (c - mu) * lax.rsqrt(var + 1e-5) * lnw_ref[...] + lnb_ref[...]
    y_ref[...] = y * _sigmoid(y)


def _conv(zb, bsz, seq, bias, dw, dwb, lnw, lnb, tm):
    t, two_bw = zb.shape
    bw = two_bw // 2
    nt = seq // tm
    row = lambda b, j: (b * nt + j, 0)
    return pl.pallas_call(
        functools.partial(_conv_body, bw),
        out_shape=jax.ShapeDtypeStruct((t, bw), F32),
        grid=(bsz, nt),
        in_specs=[pl.BlockSpec((tm, two_bw), row), _const_spec((1, two_bw)), _const_spec((CONV_HALO, bw)),
                  _const_spec((1, bw)), _const_spec((1, bw)), _const_spec((1, bw))],
        out_specs=pl.BlockSpec((tm, bw), row),
        scratch_shapes=[pltpu.VMEM((tm + CONV_HALO, bw), F32), pltpu.VMEM((7, tm + CONV_HALO - 8, bw), F32)],
        compiler_params=_params("parallel", "arbitrary"),
        name="conv_module",
    )(zb, bias, dw, dwb, lnw, lnb)


def _wkv_body(r_ref, lw_ref, k_ref, v_ref, kk_ref, a_ref, y_ref, state_ref):
    nseq, ch, width = r_ref.shape
    n_pairs = width // LANES
    c = pl.program_id(1)

    @pl.when(c == 0)
    def _():
        state_ref[...] = jnp.zeros_like(state_ref)

    ti = lax.broadcasted_iota(jnp.int32, (ch, ch), 0)
    tj = lax.broadcasted_iota(jnp.int32, (ch, ch), 1)
    tri = (ti >= tj).astype(BF16)
    lane = lax.broadcasted_iota(jnp.int32, (ch, LANES), 1)
    head0 = lane < HEAD_DIM

    def stack(x):
        zero = jnp.zeros_like(x)
        return jnp.concatenate([jnp.where(head0, x, zero), jnp.where(head0, zero, x)], axis=0)

    n = 2 * ch
    ri = lax.broadcasted_iota(jnp.int32, (n, n), 0) & (ch - 1)
    ci = lax.broadcasted_iota(jnp.int32, (n, n), 1) & (ch - 1)
    strict = ri > ci
    incl = ri >= ci
    eye = (lax.broadcasted_iota(jnp.int32, (n, n), 0) == lax.broadcasted_iota(jnp.int32, (n, n), 1)).astype(F32)

    rd, ad, bd, kd, bhd, khd, vd, g_last, where = [], [], [], [], [], [], [], [], []
    for s in range(nseq):
        lw = lw_ref[s]
        lg = sum(jnp.dot(tri, part, preferred_element_type=F32) for part in _split3(lw))
        lg_last = lg[ch - 1:ch, :]
        e_pos = jnp.exp(lg)
        e_neg = jnp.exp(-lg)
        e_prev = jnp.exp(lg - lw)
        e_tail = jnp.exp(lg_last - lg)
        kk = kk_ref[s]
        kka = kk * a_ref[s]
        kmod = k_ref[s]
        r_t = (r_ref[s] * e_pos).astype(BF16)
        a_t = (-kk * e_prev).astype(BF16)
        b_t = (kka * e_neg).astype(BF16)
        k_t = (kmod * e_neg).astype(BF16)
        b_h = (kka * e_tail).astype(BF16)
        k_h = (kmod * e_tail).astype(BF16)
        v_b = v_ref[s].astype(BF16)
        g_s = jnp.exp(lg_last)
        for p in range(n_pairs):
            sl = slice(p * LANES, (p + 1) * LANES)
            rd.append(stack(r_t[:, sl]))
            ad.append(stack(a_t[:, sl]))
            bd.append(stack(b_t[:, sl]))
            kd.append(stack(k_t[:, sl]))
            bhd.append(stack(b_h[:, sl]))
            khd.append(stack(k_h[:, sl]))
            vd.append(stack(v_b[:, sl]))
            g_last.append(g_s[:, sl])
            where.append((s, p, sl))
    units = range(len(where))

    ob = [_dot_nt(jnp.concatenate([ad[u], rd[u]], axis=0), bd[u]) for u in units]
    ok = [_dot_nt(jnp.concatenate([ad[u], rd[u]], axis=0), kd[u]) for u in units]
    a_ab = [jnp.where(strict, ob[u][:n], 0.0) for u in units]
    a_rb = [jnp.where(incl, ob[u][n:], 0.0).astype(BF16) for u in units]
    a_ak = [jnp.where(strict, ok[u][:n], 0.0) for u in units]
    a_rk = [jnp.where(incl, ok[u][n:], 0.0).astype(BF16) for u in units]
    akv = [_dot(a_ak[u], vd[u]) for u in units]

    tinv = [eye + a_ab[u] for u in units]
    pw = [_dot(a_ab[u], a_ab[u]) for u in units]
    for _ in range(int(math.log2(ch)) - 2):
        both = [_dot(jnp.concatenate([tinv[u], pw[u]], axis=0), pw[u]) for u in units]
        tinv = [tinv[u] + both[u][:n] for u in units]
        pw = [both[u][n:] for u in units]
    tinv = [tinv[u] + _dot(tinv[u], pw[u]) for u in units]

    w12 = [_dot(tinv[u], jnp.concatenate([ad[u], akv[u].astype(BF16)], axis=1)) for u in units]
    s_old = [state_ref[s, p] for s, p, _ in where]
    from_state = [_dot_nt(jnp.concatenate([w12[u][:, :LANES].astype(BF16), rd[u]], axis=0), s_old[u]) for u in units]
    uv = [jnp.concatenate([(from_state[u][:n] + w12[u][:, LANES:]).astype(BF16), vd[u]], axis=0) for u in units]
    y = [from_state[u][n:] + _dot(jnp.concatenate([a_rb[u], a_rk[u]], axis=1), uv[u]) for u in units]
    for u, (s, p, sl) in enumerate(where):
        y_ref[s, :, sl] = y[u][:ch] + y[u][ch:]
        state_ref[s, p] = s_old[u] * g_last[u] + _dot_tn(uv[u], jnp.concatenate([bhd[u], khd[u]], axis=0))


WKV_SEQS = 2


def _wkv(r, lw, k, v, kk, a, bsz, seq):
    width = r.shape[-1]
    nc = seq // CHUNK
    nseq = WKV_SEQS if bsz % WKV_SEQS == 0 else 1
    spec = pl.BlockSpec((nseq, CHUNK, width), lambda b, c: (b, c, 0))
    args = [t.reshape(bsz, seq, width) for t in (r, lw, k, v, kk, a)]
    y = pl.pallas_call(
        _wkv_body,
        out_shape=jax.ShapeDtypeStruct((bsz, seq, width), F32),
        grid=(bsz // nseq, nc),
        in_specs=[spec] * 6,
        out_specs=spec,
        scratch_shapes=[pltpu.VMEM((nseq, width // LANES, LANES, LANES), F32)],
        compiler_params=_params("parallel", "arbitrary"),
        name="wkv7_chunk",
    )(*args)
    return y.reshape(bsz * seq, width)


def _even_out_body(aw, y_ref, bonus_ref, g_ref, yb_ref, x_ref, lnw_ref, lnb_ref, sel_ref, w_ref, o_ref):
    sel = sel_ref[...]
    y = y_ref[...]
    inv_n = 1.0 / HEAD_DIM
    d = y - _dot_sel(y, sel) * inv_n
    var = _dot_sel(d * d, sel) * inv_n
    yn = d * lax.rsqrt(var + GN_EPS) * lnw_ref[...] + lnb_ref[...]
    ya = ((yn + bonus_ref[...]) * g_ref[...]).astype(BF16)
    o_ref[...] = (x_ref[...]
                  + jnp.dot(ya, w_ref[0:aw, :], preferred_element_type=F32)
                  + jnp.dot(yb_ref[...].astype(BF16), w_ref[aw:, :], preferred_element_type=F32))


def _even_out(y, bonus, g, yb, x2, lnw, lnb, sel, w, tm):
    t, d = x2.shape
    aw = y.shape[1]
    bw = yb.shape[1]
    row = lambda i: (i, 0)
    return pl.pallas_call(
        functools.partial(_even_out_body, aw),
        out_shape=jax.ShapeDtypeStruct((t, d), F32),
        grid=(t // tm,),
        in_specs=[pl.BlockSpec((tm, aw), row), pl.BlockSpec((tm, aw), row), pl.BlockSpec((tm, aw), row),
                  pl.BlockSpec((tm, bw), row), pl.BlockSpec((tm, d), row), _const_spec((1, aw)),
                  _const_spec((1, aw)), _const_spec((aw, aw)), _const_spec((d, d))],
        out_specs=pl.BlockSpec((tm, d), row),
        compiler_params=_params("parallel"),
        name="even_out",
    )(y, bonus, g, yb, x2, lnw, lnb, sel, w)


ODD_IN_SPLIT = 2


def _odd_in_body(d, x_ref, g_ref, w_ref, qc_ref, qs_ref, kc_ref, ks_ref, sel_ref, q_ref, k_ref, vt_ref):
    tm = x_ref.shape[0]
    sub = tm // ODD_IN_SPLIT
    sel = sel_ref[...]
    lane = lax.broadcasted_iota(jnp.int32, (sub, LANES), 1)
    first_half = (lane & (HEAD_DIM - 1)) < HEAD_DIM // 2
    zs = []
    for part in range(ODD_IN_SPLIT):
        rows = slice(part * sub, (part + 1) * sub)
        h = _rms(x_ref[rows, :], g_ref[...], 1e-6).astype(BF16)
        zs.append(jnp.dot(h, w_ref[...], preferred_element_type=F32))
    for part, z in enumerate(zs):
        rows = slice(part * sub, (part + 1) * sub)
        for blk in range(d // LANES):
            sl = slice(blk * LANES, (blk + 1) * LANES)
            for off, c_ref, s_ref, out_ref in ((0, qc_ref, qs_ref, q_ref), (d, kc_ref, ks_ref, k_ref)):
                xq = z[:, off + blk * LANES:off + (blk + 1) * LANES]
                ss = jnp.dot((xq * xq).astype(BF16), sel, preferred_element_type=F32)
                swapped = jnp.where(first_half, pltpu.roll(xq, LANES - HEAD_DIM // 2, 1),
                                    pltpu.roll(xq, HEAD_DIM // 2, 1))
                out = lax.rsqrt(ss + HEAD_DIM * 1e-6) * (xq * c_ref[rows, :] + swapped * s_ref[rows, :])
                out_ref[rows, sl] = out.astype(BF16)
        for hd in range(d // LANES):
            vt_ref[hd, :, rows] = z[:, 2 * d + hd * LANES:2 * d + (hd + 1) * LANES].T.astype(BF16)


def _odd_in(x2, bsz, seq, g, w, tabs, sel, tm):
    t, d = x2.shape
    nt = seq // tm
    heads = d // LANES
    row = lambda b, j: (b * nt + j, 0)
    pos = lambda b, j: (j, 0)
    out = jax.ShapeDtypeStruct((t, d), BF16)
    return pl.pallas_call(
        functools.partial(_odd_in_body, d),
        out_shape=(out, out, jax.ShapeDtypeStruct((bsz, heads, nt, LANES, tm), BF16)),
        grid=(bsz, nt),
        in_specs=[pl.BlockSpec((tm, d), row), _const_spec((1, d)), _const_spec((d, 3 * d))]
                 + [pl.BlockSpec((tm, LANES), pos)] * 4 + [_const_spec((LANES, LANES))],
        out_specs=(pl.BlockSpec((tm, d), row), pl.BlockSpec((tm, d), row),
                   pl.BlockSpec((None, heads, None, LANES, tm), lambda b, j: (b, 0, j, 0, 0))),
        compiler_params=_params("parallel", "parallel"),
        name="odd_in",
    )(x2, g.reshape(1, d), w, *tabs, sel)


ATTN_TQ = 512
ATTN_TK = 512
SUM_ROWS = 16


def _attn_body(lam_init, q_ref, k_ref, vt_ref, lam_ref, sn_ref, o_ref, qd_buf, s_buf, p_buf, al_buf, m_ref, acc_ref):
    seq = q_ref.shape[0]
    tk = vt_ref.shape[-1]
    tq = qd_buf.shape[1] // 2
    per_q = tq // tk
    ones = jnp.ones((SUM_ROWS, tk), BF16)
    lv = lam_ref[...]
    lam = (jnp.exp(jnp.sum(lv[0:1] * lv[1:2], axis=-1, keepdims=True))
           - jnp.exp(jnp.sum(lv[2:3] * lv[3:4], axis=-1, keepdims=True)) + lam_init)
    lane = lax.broadcasted_iota(jnp.int32, (tq, LANES), 1)
    comp0 = lane < HEAD_DIM

    def scores(i, j, slot):
        if isinstance(j, int) and j == 0:
            q = q_ref[i * tq:(i + 1) * tq, :]
            zero = jnp.zeros_like(q)
            qd_buf[i & 1] = jnp.concatenate([jnp.where(comp0, q, zero), jnp.where(comp0, zero, q)], axis=0)
            m_ref[i & 1] = jnp.full(m_ref.shape[1:], -1e30, F32)
            acc_ref[i & 1] = jnp.zeros(acc_ref.shape[1:], F32)
        start = j * tk if isinstance(j, int) else pl.multiple_of(j * tk, tk)
        kj = k_ref[pl.ds(start, tk), :]
        s_buf[slot] = lax.dot_general(kj, qd_buf[i & 1], (((1,), (1,)), ((), ())),
                                      preferred_element_type=F32)

    def softmax(i, j, slot):
        s = s_buf[slot]
        first_diag = per_q * i
        if isinstance(j, int) and j >= first_diag:
            ki = lax.broadcasted_iota(jnp.int32, s.shape, 0) + (j - first_diag) * tk
            qi = lax.broadcasted_iota(jnp.int32, s.shape, 1) & (tq - 1)
            s = jnp.where(ki <= qi, s, -jnp.inf)
        m_old = m_ref[i & 1]
        m_new = jnp.maximum(m_old, jnp.max(s, axis=0, keepdims=True))
        m_ref[i & 1] = m_new
        al_buf[slot] = jnp.exp2(m_old - m_new)
        p_buf[slot] = jnp.exp2(s - m_new).astype(BF16)

    def values(i, j, slot):
        vj = jnp.concatenate([vt_ref[j], ones], axis=0)
        acc_ref[i & 1] = al_buf[slot] * acc_ref[i & 1] + jnp.dot(vj, p_buf[slot], preferred_element_type=F32)
        if isinstance(j, int) and j == per_q * (i + 1) - 1:
            acc = acc_ref[i & 1]
            o = acc[:LANES] / acc[LANES:LANES + 1]
            out = (o[:, :tq] - lam * o[:, tq:]).T
            out = _rms(out, sn_ref[...], 1e-5) * (1.0 - lam_init)
            o_ref[i * tq:(i + 1) * tq, :] = out.astype(BF16)

    blocks = [(i, j) for i in range(seq // tq) for j in range(per_q * (i + 1))]

    def step(t, j_of=lambda j: j):
        for stage, lag in ((values, 2), (softmax, 1), (scores, 0)):
            if 0 <= t - lag < len(blocks):
                i, j = blocks[t - lag]
                stage(i, j_of(j), (t - lag) & 1)

    def steady(t):
        if t < 2 or t >= len(blocks):
            return False
        (i0, _), (i1, j1), (i2, _) = blocks[t - 2], blocks[t - 1], blocks[t]
        return i0 == i1 == i2 and j1 < per_q * i1

    t = 0
    while t < len(blocks) + 2:
        run = 0
        while steady(t + run):
            run += 1
        pairs = run // 2
        if pairs >= 2:
            def body(u, _, t=t):
                step(t, lambda j: j + 2 * u)
                step(t + 1, lambda j: j + 2 * u)
                return 0
            lax.fori_loop(0, pairs, body, 0)
            t += 2 * pairs
        else:
            step(t)
            t += 1


def _attn(q3, k3, vt5, lam_vecs, sub_norm, lam_init, tq):
    bsz, seq, d = q3.shape
    heads = d // LANES
    nkv, _, tk = vt5.shape[2:]
    assert tq % tk == 0 and seq % tq == 0
    head_rows = pl.BlockSpec((None, seq, LANES), lambda b, h: (b, 0, h))
    return pl.pallas_call(
        functools.partial(_attn_body, lam_init),
        out_shape=jax.ShapeDtypeStruct((bsz, seq, d), BF16),
        grid=(bsz, heads),
        in_specs=[head_rows, head_rows,
                  pl.BlockSpec((None, None, nkv, LANES, tk), lambda b, h: (b, h, 0, 0, 0)),
                  _const_spec((4, HEAD_DIM)), _const_spec((1, LANES))],
        out_specs=head_rows,
        scratch_shapes=[pltpu.VMEM((2, 2 * tq, LANES), BF16), pltpu.VMEM((2, tk, 2 * tq), F32),
                        pltpu.VMEM((2, tk, 2 * tq), BF16), pltpu.VMEM((2, 1, 2 * tq), F32),
                        pltpu.VMEM((2, 1, 2 * tq), F32), pltpu.VMEM((2, LANES + SUM_ROWS, 2 * tq), F32)],
        compiler_params=_params("parallel", "parallel"),
        name="diff_attn",
    )(q3, k3, vt5, lam_vecs, sub_norm)


def _proj_res_body(o_ref, x_ref, w_ref, out_ref):
    out_ref[...] = x_ref[...] + jnp.dot(o_ref[...], w_ref[...], preferred_element_type=F32)


def _proj_res(o2, x2, w, tm):
    t, d = x2.shape
    row = lambda i: (i, 0)
    return pl.pallas_call(
        _proj_res_body,
        out_shape=jax.ShapeDtypeStruct((t, d), F32),
        grid=(t // tm,),
        in_specs=[pl.BlockSpec((tm, d), row), pl.BlockSpec((tm, d), row), _const_spec((d, d))],
        out_specs=pl.BlockSpec((tm, d), row),
        compiler_params=_params("parallel"),
        name="proj_res",
    )(o2, x2, w)


def _block_ones(n, blk):
    idx = jnp.arange(n) // blk
    return (idx[:, None] == idx[None, :]).astype(BF16)


def _rope_tables(seq):
    inv = 1.0 / (ROPE_THETA ** (jnp.arange(0, HEAD_DIM, 2, dtype=F32) / HEAD_DIM))
    ang = jnp.arange(seq, dtype=F32)[:, None] * inv[None, :]
    cos, sin = jnp.cos(ang), jnp.sin(ang)
    reps = LANES // HEAD_DIM
    return jnp.tile(jnp.concatenate([cos, cos], -1), (1, reps)), jnp.tile(jnp.concatenate([-sin, sin], -1), (1, reps))


def _qk_tables(cos, sin, norm, scale):
    c = scale * math.sqrt(HEAD_DIM)
    reps = LANES // HEAD_DIM
    gain = jnp.tile(norm, reps) * c
    gain_swapped = jnp.tile(jnp.roll(norm, HEAD_DIM // 2), reps) * c
    return cos * gain[None, :], sin * gain_swapped[None, :]


def _row_tile(seq):
    return min(256, seq)


def kernel(x, ffn_norm, ffn_w_gate, ffn_w_up, ffn_w_down, mix_norm, a_w_in, a_mu, a_w0, a_w2, a_a0, a_a2, a_g2, a_k_k, a_k_a, a_r_k, a_ln_w, a_ln_b, b_glu_bias, b_dw, b_dw_bias, b_ln_w, b_ln_b, e_w_out, c_w_in, c_q_norm, c_k_norm, c_lq1, c_lk1, c_lq2, c_lk2, c_sub_norm, c_w_out):
    bsz, seq, d = x.shape
    depth = ffn_norm.shape[0]
    t = bsz * seq
    tm = _row_tile(seq)
    tm_proj = min(FFN_TM, seq)
    x2 = x.reshape(t, d)
    wg, wu, wd = ffn_w_gate.astype(BF16), ffn_w_up.astype(BF16), ffn_w_down.astype(BF16)
    cos, sin = _rope_tables(seq)

    def ffn(x2, layer, which):
        return _ffn(x2, ffn_norm[layer, which], wg[layer, which], wu[layer, which], wd[layer, which],
                    min(FFN_TM, seq))

    for layer in range(depth):
        x2 = ffn(x2, layer, 0)
        j = layer // 2
        if layer % 2 == 0:
            aw = a_w0.shape[1]
            a_proj = a_mu.shape[1]
            row = lambda v: v.reshape(1, -1)
            sel = _block_ones(aw, HEAD_DIM)
            za, zb = _even_in(x2, mix_norm[layer], a_w_in[j].astype(BF16), a_proj, tm_proj)
            w2p = jnp.pad(a_w2[j], ((0, ICL_RANK), (0, 0)))
            w2h = w2p.astype(BF16)
            w2l = (w2p - w2h.astype(F32)).astype(BF16)
            a2p = jnp.pad(a_a2[j], ((DECAY_RANK, 0), (0, 0))).astype(BF16)
            r, lw, k, v, kk, a, g, bonus = _rwkv_prep(
                za, bsz, seq, aw, row(a_mu[j]), row(a_w0[j]), w2h, w2l, row(a_a0[j]), a2p, a_g2[j].astype(BF16),
                row(a_k_k[j]), row(a_k_a[j]), row(a_r_k[j]), sel, tm)
            yb = _conv(zb, bsz, seq, row(b_glu_bias[j]), jnp.pad(b_dw[j], ((0, CONV_HALO - CONV_WIDTH), (0, 0))),
                       row(b_dw_bias[j]), row(b_ln_w[j]), row(b_ln_b[j]), tm)
            y = _wkv(r, lw, k, v, kk, a, bsz, seq)
            x2 = _even_out(y, bonus, g, yb, x2, row(a_ln_w[j]), row(a_ln_b[j]), sel, e_w_out[j].astype(BF16),
                           tm_proj)
        else:
            lam_init = 0.8 - 0.6 * math.exp(-0.3 * layer)
            tabs = (_qk_tables(cos, sin, c_q_norm[j], ATTN_SCALE * math.log2(math.e))
                    + _qk_tables(cos, sin, c_k_norm[j], 1.0))
            q, k, vt5 = _odd_in(x2, bsz, seq, mix_norm[layer], c_w_in[j].astype(BF16), tabs,
                                _block_ones(LANES, HEAD_DIM), min(ATTN_TK, seq))
            lam_vecs = jnp.stack([c_lq1[j], c_lk1[j], c_lq2[j], c_lk2[j]])
            o = _attn(q.reshape(bsz, seq, d), k.reshape(bsz, seq, d), vt5, lam_vecs,
                      c_sub_norm[j].reshape(1, LANES), lam_init, ATTN_TQ)
            x2 = _proj_res(o.reshape(t, d), x2, c_w_out[j].astype(BF16), tm_proj)
        x2 = ffn(x2, layer, 1)
    return x2.reshape(bsz, seq, d)
```

```python
import functools
import math

import jax
import jax.numpy as jnp
from jax import lax
from jax.experimental import pallas as pl
from jax.experimental.pallas import tpu as pltpu

F32 = jnp.float32
BF16 = jnp.bfloat16

HEAD_DIM = 64
LANES = 128
CONV_WIDTH = 31
CONV_HALO = 32
DECAY_RANK = 64
ICL_RANK = 64
GATE_RANK = 128
GN_EPS = 64e-5
ROPE_THETA = 10000.0
ATTN_SCALE = HEAD_DIM ** -0.5
CHUNK = 64
VMEM_LIMIT = 56 * 1024 * 1024


def _const_spec(shape):
    return pl.BlockSpec(shape, lambda *_: (0,) * len(shape), pipeline_mode=pl.Buffered(1))


def _params(*sem):
    return pltpu.CompilerParams(dimension_semantics=sem, vmem_limit_bytes=VMEM_LIMIT)


def _dot(a, b):
    return jnp.dot(a.astype(BF16), b.astype(BF16), preferred_element_type=F32)


def _dot_nt(a, b):
    return lax.dot_general(a.astype(BF16), b.astype(BF16), (((1,), (1,)), ((), ())),
                           preferred_element_type=F32)


def _dot_tn(a, b):
    return lax.dot_general(a.astype(BF16), b.astype(BF16), (((0,), (0,)), ((), ())),
                           preferred_element_type=F32)


def _split2(x):
    hi = x.astype(BF16)
    lo = (x - hi.astype(F32)).astype(BF16)
    return hi, lo


def _split3(x):
    hi = x.astype(BF16)
    r1 = x - hi.astype(F32)
    mid = r1.astype(BF16)
    lo = (r1 - mid.astype(F32)).astype(BF16)
    return hi, mid, lo


def _dot_sel(x, sel):
    return jnp.dot(x.astype(BF16), sel, preferred_element_type=F32)


def _dot_x3(a, b_hi, b_lo):
    a_hi, a_lo = _split2(a)
    return (jnp.dot(a_hi, b_hi, preferred_element_type=F32)
            + jnp.dot(a_lo, b_hi, preferred_element_type=F32)
            + jnp.dot(a_hi, b_lo, preferred_element_type=F32))


def _rms(x, g, eps):
    return x * lax.rsqrt(jnp.mean(x * x, axis=-1, keepdims=True) + eps) * g


def _sigmoid(x):
    return 1.0 / (1.0 + jnp.exp(-x))


FFN_TM = 512


def _swiglu_step(x, g_ref, wg_ref, wu_ref, wd_ref):
    h = _rms(x, g_ref[...], 1e-6).astype(BF16)
    gate = jnp.dot(h, wg_ref[...], preferred_element_type=F32)
    up = jnp.dot(h, wu_ref[...], preferred_element_type=F32)
    act = (gate * _sigmoid(gate) * up).astype(BF16)
    return x + 0.5 * jnp.dot(act, wd_ref[...], preferred_element_type=F32)


def _ffn_body(x_ref, g_ref, wg_ref, wu_ref, wd_ref, o_ref):
    o_ref[...] = _swiglu_step(x_ref[...], g_ref, wg_ref, wu_ref, wd_ref)


def _ffn_specs(d, f):
    return [_const_spec((1, d)), _const_spec((d, f)), _const_spec((d, f)), _const_spec((f, d))]


def _ffn(x2, ffn_w, tm):
    t, d = x2.shape
    row = pl.BlockSpec((tm, d), lambda i: (i, 0))
    return pl.pallas_call(
        _ffn_body,
        out_shape=jax.ShapeDtypeStruct((t, d), F32),
        grid=(t // tm,),
        in_specs=[row] + _ffn_specs(d, ffn_w[1].shape[1]),
        out_specs=row,
        compiler_params=_params("parallel"),
        name="ffn",
    )(x2, *ffn_w)


def _even_in_body(a_proj, x_ref, g_ref, w_ref, za_ref, zb_ref):
    h = _rms(x_ref[...], g_ref[...], 1e-6).astype(BF16)
    z = jnp.dot(h, w_ref[...], preferred_element_type=F32)
    za_ref[...] = z[:, :a_proj]
    zb_ref[...] = z[:, a_proj:]


def _even_in(x2, g, w, a_proj, tm):
    t, d = x2.shape
    n = w.shape[1]
    return pl.pallas_call(
        functools.partial(_even_in_body, a_proj),
        out_shape=(jax.ShapeDtypeStruct((t, a_proj), F32), jax.ShapeDtypeStruct((t, n - a_proj), F32)),
        grid=(t // tm,),
        in_specs=[pl.BlockSpec((tm, d), lambda i: (i, 0)), _const_spec((1, d)), _const_spec((d, n))],
        out_specs=(pl.BlockSpec((tm, a_proj), lambda i: (i, 0)),
                   pl.BlockSpec((tm, n - a_proj), lambda i: (i, 0))),
        compiler_params=_params("parallel"),
        name="even_in",
    )(x2, g.reshape(1, d), w)


def _rwkv_prep_body(aw, za_ref, mu_ref, w0_ref, w2h_ref, w2l_ref, a0_ref, a2_ref, g2_ref, kk_ref_w, ka_ref,
                    rk_ref, sel_ref, r_out, lw_out, k_out, v_out, kk_out, a_out, g_out, bonus_out, prev_ref):
    tm = za_ref.shape[0]
    j = pl.program_id(1)

    @pl.when(j == 0)
    def _():
        prev_ref[...] = jnp.zeros_like(prev_ref)

    za = za_ref[...]
    row = lax.broadcasted_iota(jnp.int32, za.shape, 0)
    shifted = jnp.where(row == 0, prev_ref[0:1, :], pltpu.roll(za, 1, 0))
    prev_ref[0:1, :] = za[tm - 1:tm, :]
    xr = za + (shifted - za) * mu_ref[...]

    r = xr[:, 0:aw]
    k = xr[:, aw:2 * aw]
    v = xr[:, 2 * aw:3 * aw]
    lr = xr[:, 3 * aw:3 * aw + DECAY_RANK + ICL_RANK]
    gd = xr[:, 3 * aw + DECAY_RANK + ICL_RANK:]

    pre = w0_ref[...] + _dot_x3(jnp.tanh(lr), w2h_ref[...], w2l_ref[...])
    softplus_neg = jnp.maximum(-pre, 0.0) + jnp.log(1.0 + jnp.exp(-jnp.abs(pre)))
    w_log = -softplus_neg - 0.5
    lw_out[...] = -jnp.exp(w_log)

    a = _sigmoid(a0_ref[...] + jnp.dot(lr.astype(BF16), a2_ref[...], preferred_element_type=F32))
    g_out[...] = jnp.dot(_sigmoid(gd).astype(BF16), g2_ref[...], preferred_element_type=F32)

    sel = sel_ref[...]
    kk = k * kk_ref_w[...]
    kk = kk / jnp.maximum(jnp.sqrt(_dot_sel(kk * kk, sel)), 1e-12)
    kmod = k * (1.0 + (a - 1.0) * ka_ref[...])
    bonus_out[...] = _dot_sel(r * kmod * rk_ref[...], sel) * v
    r_out[...] = r
    k_out[...] = kmod
    v_out[...] = v
    kk_out[...] = kk
    a_out[...] = a


def _rwkv_prep(za, bsz, seq, aw, mu, w0, w2h, w2l, a0, a2p, g2, k_k, k_a, r_k, sel, tm):
    t, ap = za.shape
    nt = seq // tm
    row = lambda b, j: (b * nt + j, 0)
    vec = lambda n: _const_spec((1, n))
    out = jax.ShapeDtypeStruct((t, aw), F32)
    lora = DECAY_RANK + ICL_RANK
    return pl.pallas_call(
        functools.partial(_rwkv_prep_body, aw),
        out_shape=(out,) * 8,
        grid=(bsz, nt),
        in_specs=[pl.BlockSpec((tm, ap), row), vec(ap), vec(aw), _const_spec((lora, aw)), _const_spec((lora, aw)),
                  vec(aw), _const_spec((lora, aw)), _const_spec((GATE_RANK, aw)), vec(aw), vec(aw), vec(aw),
                  _const_spec((aw, aw))],
        out_specs=(pl.BlockSpec((tm, aw), row),) * 8,
        scratch_shapes=[pltpu.VMEM((8, ap), F32)],
        compiler_params=_params("parallel", "arbitrary"),
        name="rwkv_prep",
    )(za, mu, w0, w2h, w2l, a0, a2p, g2, k_k, k_a, r_k, sel)


def _conv_body(bw, zb_ref, bias_ref, dw_ref, dwb_ref, lnw_ref, lnb_ref, y_ref, buf_ref, shift_ref):
    tm = zb_ref.shape[0]
    j = pl.program_id(1)
    u = zb_ref[...] + bias_ref[...]
    gl = u[:, :bw] * _sigmoid(u[:, bw:])

    @pl.when(j == 0)
    def _():
        buf_ref[0:CONV_HALO, :] = jnp.zeros((CONV_HALO, bw), F32)

    @pl.when(j > 0)
    def _():
        buf_ref[0:CONV_HALO, :] = buf_ref[tm:tm + CONV_HALO, :]

    buf_ref[CONV_HALO:, :] = gl
    base = CONV_HALO - (CONV_WIDTH - 1)
    acc = jnp.zeros((tm, bw), F32)
    for shift in range(8):
        taps = [tap for tap in range(CONV_WIDTH) if (base + tap) % 8 == shift]
        if not taps:
            continue
        span = (base + taps[-1]) // 8 * 8
        if shift:
            shift_ref[shift - 1, 0:tm + span, :] = buf_ref[pl.ds(shift, tm + span), :]
        for tap in taps:
            off = (base + tap) // 8 * 8
            rows = shift_ref[shift - 1, off:off + tm, :] if shift else buf_ref[off:off + tm, :]
            acc = acc + dw_ref[tap:tap + 1, :] * rows
    c = acc + dwb_ref[...]
    mu = jnp.mean(c, axis=-1, keepdims=True)
    var = jnp.mean(jnp.square(c - mu), axis=-1, keepdims=True)
    y = (c - mu) * lax.rsqrt(var + 1e-5) * lnw_ref[...] + lnb_ref[...]
    y_ref[...] = y * _sigmoid(y)


def _conv(zb, bsz, seq, bias, dw, dwb, lnw, lnb, tm):
    t, two_bw = zb.shape
    bw = two_bw // 2
    nt = seq // tm
    row = lambda b, j: (b * nt + j, 0)
    return pl.pallas_call(
        functools.partial(_conv_body, bw),
        out_shape=jax.ShapeDtypeStruct((t, bw), F32),
        grid=(bsz, nt),
        in_specs=[pl.BlockSpec((tm, two_bw), row), _const_spec((1, two_bw)), _const_spec((CONV_HALO, bw)),
                  _const_spec((1, bw)), _const_spec((1, bw)), _const_spec((1, bw))],
        out_specs=pl.BlockSpec((tm, bw), row),
        scratch_shapes=[pltpu.VMEM((tm + CONV_HALO, bw), F32), pltpu.VMEM((7, tm + CONV_HALO - 8, bw), F32)],
        compiler_params=_params("parallel", "arbitrary"),
        name="conv_module",
    )(zb, bias, dw, dwb, lnw, lnb)


def _wkv_body(r_ref, lw_ref, k_ref, v_ref, kk_ref, a_ref, y_ref, state_ref):
    nseq, ch, width = r_ref.shape
    n_pairs = width // LANES
    c = pl.program_id(1)

    @pl.when(c == 0)
    def _():
        state_ref[...] = jnp.zeros_like(state_ref)

    ti = lax.broadcasted_iota(jnp.int32, (ch, ch), 0)
    tj = lax.broadcasted_iota(jnp.int32, (ch, ch), 1)
    tri = (ti >= tj).astype(BF16)
    lane = lax.broadcasted_iota(jnp.int32, (ch, LANES), 1)
    head0 = lane < HEAD_DIM

    def stack(x):
        zero = jnp.zeros_like(x)
        return jnp.concatenate([jnp.where(head0, x, zero), jnp.where(head0, zero, x)], axis=0)

    n = 2 * ch
    ri = lax.broadcasted_iota(jnp.int32, (n, n), 0) & (ch - 1)
    ci = lax.broadcasted_iota(jnp.int32, (n, n), 1) & (ch - 1)
    strict = ri > ci
    incl = ri >= ci
    eye = (lax.broadcasted_iota(jnp.int32, (n, n), 0) == lax.broadcasted_iota(jnp.int32, (n, n), 1)).astype(F32)

    rd, ad, bd, kd, bhd, khd, vd, g_last, where = [], [], [], [], [], [], [], [], []
    for s in range(nseq):
        lw = lw_ref[s]
        lg = sum(jnp.dot(tri, part, preferred_element_type=F32) for part in _split3(lw))
        lg_last = lg[ch - 1:ch, :]
        e_pos = jnp.exp(lg)
        e_neg = jnp.exp(-lg)
        e_prev = jnp.exp(lg - lw)
        e_tail = jnp.exp(lg_last - lg)
        kk = kk_ref[s]
        kka = kk * a_ref[s]
        kmod = k_ref[s]
        r_t = (r_ref[s] * e_pos).astype(BF16)
        a_t = (-kk * e_prev).astype(BF16)
        b_t = (kka * e_neg).astype(BF16)
        k_t = (kmod * e_neg).astype(BF16)
        b_h = (kka * e_tail).astype(BF16)
        k_h = (kmod * e_tail).astype(BF16)
        v_b = v_ref[s].astype(BF16)
        g_s = jnp.exp(lg_last)
        for p in range(n_pairs):
            sl = slice(p * LANES, (p + 1) * LANES)
            rd.append(stack(r_t[:, sl]))
            ad.append(stack(a_t[:, sl]))
            bd.append(stack(b_t[:, sl]))
            kd.append(stack(k_t[:, sl]))
            bhd.append(stack(b_h[:, sl]))
            khd.append(stack(k_h[:, sl]))
            vd.append(stack(v_b[:, sl]))
            g_last.append(g_s[:, sl])
            where.append((s, p, sl))
    units = range(len(where))

    ob = [_dot_nt(jnp.concatenate([ad[u], rd[u]], axis=0), bd[u]) for u in units]
    ok = [_dot_nt(jnp.concatenate([ad[u], rd[u]], axis=0), kd[u]) for u in units]
    a_ab = [jnp.where(strict, ob[u][:n], 0.0) for u in units]
    a_rb = [jnp.where(incl, ob[u][n:], 0.0).astype(BF16) for u in units]
    a_ak = [jnp.where(strict, ok[u][:n], 0.0) for u in units]
    a_rk = [jnp.where(incl, ok[u][n:], 0.0).astype(BF16) for u in units]
    akv = [_dot(a_ak[u], vd[u]) for u in units]

    tinv = [eye + a_ab[u] for u in units]
    pw = [_dot(a_ab[u], a_ab[u]) for u in units]
    for _ in range(int(math.log2(ch)) - 2):
        both = [_dot(jnp.concatenate([tinv[u], pw[u]], axis=0), pw[u]) for u in units]
        tinv = [tinv[u] + both[u][:n] for u in units]
        pw = [both[u][n:] for u in units]
    tinv = [tinv[u] + _dot(tinv[u], pw[u]) for u in units]

    w12 = [_dot(tinv[u], jnp.concatenate([ad[u], akv[u].astype(BF16)], axis=1)) for u in units]
    s_old = [state_ref[s, p] for s, p, _ in where]
    from_state = [_dot_nt(jnp.concatenate([w12[u][:, :LANES].astype(BF16), rd[u]], axis=0), s_old[u]) for u in units]
    uv = [jnp.concatenate([(from_state[u][:n] + w12[u][:, LANES:]).astype(BF16), vd[u]], axis=0) for u in units]
    y = [from_state[u][n:] + _dot(jnp.concatenate([a_rb[u], a_rk[u]], axis=1), uv[u]) for u in units]
    for u, (s, p, sl) in enumerate(where):
        y_ref[s, :, sl] = y[u][:ch] + y[u][ch:]
        state_ref[s, p] = s_old[u] * g_last[u] + _dot_tn(uv[u], jnp.concatenate([bhd[u], khd[u]], axis=0))


WKV_SEQS = 2


def _wkv(r, lw, k, v, kk, a, bsz, seq):
    width = r.shape[-1]
    nc = seq // CHUNK
    nseq = WKV_SEQS if bsz % WKV_SEQS == 0 else 1
    spec = pl.BlockSpec((nseq, CHUNK, width), lambda b, c: (b, c, 0))
    args = [t.reshape(bsz, seq, width) for t in (r, lw, k, v, kk, a)]
    y = pl.pallas_call(
        _wkv_body,
        out_shape=jax.ShapeDtypeStruct((bsz, seq, width), F32),
        grid=(bsz // nseq, nc),
        in_specs=[spec] * 6,
        out_specs=spec,
        scratch_shapes=[pltpu.VMEM((nseq, width // LANES, LANES, LANES), F32)],
        compiler_params=_params("parallel", "arbitrary"),
        name="wkv7_chunk",
    )(*args)
    return y.reshape(bsz * seq, width)


def _even_out_ffn_body(aw, y_ref, bonus_ref, gate_ref, yb_ref, x_ref, lnw_ref, lnb_ref, sel_ref, w_ref,
                       g_ref, wg_ref, wu_ref, wd_ref, o_ref):
    sel = sel_ref[...]
    y = y_ref[...]
    inv_n = 1.0 / HEAD_DIM
    d = y - _dot_sel(y, sel) * inv_n
    var = _dot_sel(d * d, sel) * inv_n
    yn = d * lax.rsqrt(var + GN_EPS) * lnw_ref[...] + lnb_ref[...]
    ya = ((yn + bonus_ref[...]) * gate_ref[...]).astype(BF16)
    x = (x_ref[...]
         + jnp.dot(ya, w_ref[0:aw, :], preferred_element_type=F32)
         + jnp.dot(yb_ref[...].astype(BF16), w_ref[aw:, :], preferred_element_type=F32))
    o_ref[...] = _swiglu_step(x, g_ref, wg_ref, wu_ref, wd_ref)


def _even_out_ffn(y, bonus, gate, yb, x2, lnw, lnb, sel, w, ffn_w, tm):
    t, d = x2.shape
    aw = y.shape[1]
    bw = yb.shape[1]
    row = lambda i: (i, 0)
    return pl.pallas_call(
        functools.partial(_even_out_ffn_body, aw),
        out_shape=jax.ShapeDtypeStruct((t, d), F32),
        grid=(t // tm,),
        in_specs=[pl.BlockSpec((tm, aw), row), pl.BlockSpec((tm, aw), row), pl.BlockSpec((tm, aw), row),
                  pl.BlockSpec((tm, bw), row), pl.BlockSpec((tm, d), row), _const_spec((1, aw)),
                  _const_spec((1, aw)), _const_spec((aw, aw)), _const_spec((d, d))]
                 + _ffn_specs(d, ffn_w[1].shape[1]),
        out_specs=pl.BlockSpec((tm, d), row),
        compiler_params=_params("parallel"),
        name="even_out_ffn",
    )(y, bonus, gate, yb, x2, lnw, lnb, sel, w, *ffn_w)


ODD_IN_SPLIT = 2


def _odd_in_body(d, x_ref, g_ref, w_ref, qc_ref, qs_ref, kc_ref, ks_ref, sel_ref, q_ref, k_ref, vt_ref):
    tm = x_ref.shape[0]
    sub = tm // ODD_IN_SPLIT
    sel = sel_ref[...]
    lane = lax.broadcasted_iota(jnp.int32, (sub, LANES), 1)
    first_half = (lane & (HEAD_DIM - 1)) < HEAD_DIM // 2
    zs = []
    for part in range(ODD_IN_SPLIT):
        rows = slice(part * sub, (part + 1) * sub)
        h = _rms(x_ref[rows, :], g_ref[...], 1e-6).astype(BF16)
        zs.append(jnp.dot(h, w_ref[...], preferred_element_type=F32))
    for part, z in enumerate(zs):
        rows = slice(part * sub, (part + 1) * sub)
        for blk in range(d // LANES):
            sl = slice(blk * LANES, (blk + 1) * LANES)
            for off, c_ref, s_ref, out_ref in ((0, qc_ref, qs_ref, q_ref), (d, kc_ref, ks_ref, k_ref)):
                xq = z[:, off + blk * LANES:off + (blk + 1) * LANES]
                ss = jnp.dot((xq * xq).astype(BF16), sel, preferred_element_type=F32)
                swapped = jnp.where(first_half, pltpu.roll(xq, LANES - HEAD_DIM // 2, 1),
                                    pltpu.roll(xq, HEAD_DIM // 2, 1))
                out = lax.rsqrt(ss + HEAD_DIM * 1e-6) * (xq * c_ref[rows, :] + swapped * s_ref[rows, :])
                out_ref[rows, sl] = out.astype(BF16)
        for hd in range(d // LANES):
            vt_ref[hd, :, rows] = z[:, 2 * d + hd * LANES:2 * d + (hd + 1) * LANES].T.astype(BF16)


def _odd_in(x2, bsz, seq, g, w, tabs, sel, tm):
    t, d = x2.shape
    nt = seq // tm
    heads = d // LANES
    row = lambda b, j: (b * nt + j, 0)
    pos = lambda b, j: (j, 0)
    out = jax.ShapeDtypeStruct((t, d), BF16)
    return pl.pallas_call(
        functools.partial(_odd_in_body, d),
        out_shape=(out, out, jax.ShapeDtypeStruct((bsz, heads, nt, LANES, tm), BF16)),
        grid=(bsz, nt),
        in_specs=[pl.BlockSpec((tm, d), row), _const_spec((1, d)), _const_spec((d, 3 * d))]
                 + [pl.BlockSpec((tm, LANES), pos)] * 4 + [_const_spec((LANES, LANES))],
        out_specs=(pl.BlockSpec((tm, d), row), pl.BlockSpec((tm, d), row),
                   pl.BlockSpec((None, heads, None, LANES, tm), lambda b, j: (b, 0, j, 0, 0))),
        compiler_params=_params("parallel", "parallel"),
        name="odd_in",
    )(x2, g.reshape(1, d), w, *tabs, sel)


ATTN_TQ = 512
ATTN_TK = 512
SUM_ROWS = 16


def _attn_body(lam_init, q_ref, k_ref, vt_ref, lam_ref, sn_ref, o_ref, qd_buf, s_buf, p_buf, al_buf, m_ref, acc_ref):
    seq = q_ref.shape[0]
    tk = vt_ref.shape[-1]
    tq = qd_buf.shape[1] // 2
    per_q = tq // tk
    ones = jnp.ones((SUM_ROWS, tk), BF16)
    lv = lam_ref[...]
    lam = (jnp.exp(jnp.sum(lv[0:1] * lv[1:2], axis=-1, keepdims=True))
           - jnp.exp(jnp.sum(lv[2:3] * lv[3:4], axis=-1, keepdims=True)) + lam_init)
    lane = lax.broadcasted_iota(jnp.int32, (tq, LANES), 1)
    comp0 = lane < HEAD_DIM

    def scores(i, j, slot):
        if isinstance(j, int) and j == 0:
            q = q_ref[i * tq:(i + 1) * tq, :]
            zero = jnp.zeros_like(q)
            qd_buf[i & 1] = jnp.concatenate([jnp.where(comp0, q, zero), jnp.where(comp0, zero, q)], axis=0)
            m_ref[i & 1] = jnp.full(m_ref.shape[1:], -1e30, F32)
            acc_ref[i & 1] = jnp.zeros(acc_ref.shape[1:], F32)
        start = j * tk if isinstance(j, int) else pl.multiple_of(j * tk, tk)
        kj = k_ref[pl.ds(start, tk), :]
        s = lax.dot_general(kj, qd_buf[i & 1], (((1,), (1,)), ((), ())), preferred_element_type=F32)
        s_buf[slot] = s.astype(BF16)

    def softmax(i, j, slot):
        s = s_buf[slot]
        first_diag = per_q * i
        if isinstance(j, int) and j >= first_diag:
            ki = lax.broadcasted_iota(jnp.int32, s.shape, 0) + (j - first_diag) * tk
            qi = lax.broadcasted_iota(jnp.int32, s.shape, 1) & (tq - 1)
            s = jnp.where(ki <= qi, s, -jnp.inf)
        m_old = m_ref[i & 1]
        m_new = jnp.maximum(m_old, jnp.max(s, axis=0, keepdims=True).astype(F32))
        m_ref[i & 1] = m_new
        al_buf[slot] = jnp.exp2(m_old - m_new)
        p_buf[slot] = jnp.exp2(s - m_new.astype(BF16))

    def values(i, j, slot):
        vj = jnp.concatenate([vt_ref[j], ones], axis=0)
        acc_ref[i & 1] = al_buf[slot] * acc_ref[i & 1] + jnp.dot(vj, p_buf[slot], preferred_element_type=F32)
        if isinstance(j, int) and j == per_q * (i + 1) - 1:
            acc = acc_ref[i & 1]
            o = acc[:LANES] / acc[LANES:LANES + 1]
            out = (o[:, :tq] - lam * o[:, tq:]).T
            out = _rms(out, sn_ref[...], 1e-5) * (1.0 - lam_init)
            o_ref[i * tq:(i + 1) * tq, :] = out.astype(BF16)

    blocks = [(i, j) for i in range(seq // tq) for j in range(per_q * (i + 1))]

    def step(t, j_of=lambda j: j):
        for stage, lag in ((values, 2), (softmax, 1), (scores, 0)):
            if 0 <= t - lag < len(blocks):
                i, j = blocks[t - lag]
                stage(i, j_of(j), (t - lag) & 1)

    def steady(t):
        if t < 2 or t >= len(blocks):
            return False
        (i0, _), (i1, j1), (i2, _) = blocks[t - 2], blocks[t - 1], blocks[t]
        return i0 == i1 == i2 and j1 < per_q * i1

    t = 0
    while t < len(blocks) + 2:
        run = 0
        while steady(t + run):
            run += 1
        pairs = run // 2
        if pairs >= 2:
            def body(u, _, t=t):
                step(t, lambda j: j + 2 * u)
                step(t + 1, lambda j: j + 2 * u)
                return 0
            lax.fori_loop(0, pairs, body, 0)
            t += 2 * pairs
        else:
            step(t)
            t += 1


def _attn(q3, k3, vt5, lam_vecs, sub_norm, lam_init, tq):
    bsz, seq, d = q3.shape
    heads = d // LANES
    nkv, _, tk = vt5.shape[2:]
    assert tq % tk == 0 and seq % tq == 0
    head_rows = pl.BlockSpec((None, seq, LANES), lambda b, h: (b, 0, h))
    return pl.pallas_call(
        functools.partial(_attn_body, lam_init),
        out_shape=jax.ShapeDtypeStruct((bsz, seq, d), BF16),
        grid=(bsz, heads),
        in_specs=[head_rows, head_rows,
                  pl.BlockSpec((None, None, nkv, LANES, tk), lambda b, h: (b, h, 0, 0, 0)),
                  _const_spec((4, HEAD_DIM)), _const_spec((1, LANES))],
        out_specs=head_rows,
        scratch_shapes=[pltpu.VMEM((2, 2 * tq, LANES), BF16), pltpu.VMEM((2, tk, 2 * tq), BF16),
                        pltpu.VMEM((2, tk, 2 * tq), BF16), pltpu.VMEM((2, 1, 2 * tq), F32),
                        pltpu.VMEM((2, 1, 2 * tq), F32), pltpu.VMEM((2, LANES + SUM_ROWS, 2 * tq), F32)],
        compiler_params=_params("parallel", "parallel"),
        name="diff_attn",
    )(q3, k3, vt5, lam_vecs, sub_norm)


def _proj_ffn_body(o_ref, x_ref, w_ref, g_ref, wg_ref, wu_ref, wd_ref, out_ref):
    x = x_ref[...] + jnp.dot(o_ref[...], w_ref[...], preferred_element_type=F32)
    out_ref[...] = _swiglu_step(x, g_ref, wg_ref, wu_ref, wd_ref)


def _proj_ffn(o2, x2, w, ffn_w, tm):
    t, d = x2.shape
    row = pl.BlockSpec((tm, d), lambda i: (i, 0))
    return pl.pallas_call(
        _proj_ffn_body,
        out_shape=jax.ShapeDtypeStruct((t, d), F32),
        grid=(t // tm,),
        in_specs=[row, row, _const_spec((d, d))] + _ffn_specs(d, ffn_w[1].shape[1]),
        out_specs=row,
        compiler_params=_params("parallel"),
        name="proj_ffn",
    )(o2, x2, w, *ffn_w)


def _block_ones(n, blk):
    idx = jnp.arange(n) // blk
    return (idx[:, None] == idx[None, :]).astype(BF16)


def _rope_tables(seq):
    inv = 1.0 / (ROPE_THETA ** (jnp.arange(0, HEAD_DIM, 2, dtype=F32) / HEAD_DIM))
    ang = jnp.arange(seq, dtype=F32)[:, None] * inv[None, :]
    cos, sin = jnp.cos(ang), jnp.sin(ang)
    reps = LANES // HEAD_DIM
    return jnp.tile(jnp.concatenate([cos, cos], -1), (1, reps)), jnp.tile(jnp.concatenate([-sin, sin], -1), (1, reps))


def _qk_tables(cos, sin, norm, scale):
    c = scale * math.sqrt(HEAD_DIM)
    reps = LANES // HEAD_DIM
    gain = jnp.tile(norm, reps) * c
    gain_swapped = jnp.tile(jnp.roll(norm, HEAD_DIM // 2), reps) * c
    return cos * gain[None, :], sin * gain_swapped[None, :]


def _row_tile(seq):
    return min(256, seq)


def kernel(x, ffn_norm, ffn_w_gate, ffn_w_up, ffn_w_down, mix_norm, a_w_in, a_mu, a_w0, a_w2, a_a0, a_a2, a_g2, a_k_k, a_k_a, a_r_k, a_ln_w, a_ln_b, b_glu_bias, b_dw, b_dw_bias, b_ln_w, b_ln_b, e_w_out, c_w_in, c_q_norm, c_k_norm, c_lq1, c_lk1, c_lq2, c_lk2, c_sub_norm, c_w_out):
    bsz, seq, d = x.shape
    depth = ffn_norm.shape[0]
    t = bsz * seq
    tm = _row_tile(seq)
    tm_proj = min(FFN_TM, seq)
    x2 = x.reshape(t, d)
    wg, wu, wd = ffn_w_gate.astype(BF16), ffn_w_up.astype(BF16), ffn_w_down.astype(BF16)
    cos, sin = _rope_tables(seq)

    def ffn_w(layer, which):
        return (ffn_norm[layer, which].reshape(1, d), wg[layer, which], wu[layer, which], wd[layer, which])

    for layer in range(depth):
        x2 = _ffn(x2, ffn_w(layer, 0), tm_proj)
        j = layer // 2
        if layer % 2 == 0:
            aw = a_w0.shape[1]
            a_proj = a_mu.shape[1]
            row = lambda v: v.reshape(1, -1)
            sel = _block_ones(aw, HEAD_DIM)
            za, zb = _even_in(x2, mix_norm[layer], a_w_in[j].astype(BF16), a_proj, tm_proj)
            w2p = jnp.pad(a_w2[j], ((0, ICL_RANK), (0, 0)))
            w2h = w2p.astype(BF16)
            w2l = (w2p - w2h.astype(F32)).astype(BF16)
            a2p = jnp.pad(a_a2[j], ((DECAY_RANK, 0), (0, 0))).astype(BF16)
            r, lw, k, v, kk, a, g, bonus = _rwkv_prep(
                za, bsz, seq, aw, row(a_mu[j]), row(a_w0[j]), w2h, w2l, row(a_a0[j]), a2p, a_g2[j].astype(BF16),
                row(a_k_k[j]), row(a_k_a[j]), row(a_r_k[j]), sel, tm)
            yb = _conv(zb, bsz, seq, row(b_glu_bias[j]), jnp.pad(b_dw[j], ((0, CONV_HALO - CONV_WIDTH), (0, 0))),
                       row(b_dw_bias[j]), row(b_ln_w[j]), row(b_ln_b[j]), tm)
            y = _wkv(r, lw, k, v, kk, a, bsz, seq)
            x2 = _even_out_ffn(y, bonus, g, yb, x2, row(a_ln_w[j]), row(a_ln_b[j]), sel, e_w_out[j].astype(BF16),
                               ffn_w(layer, 1), tm_proj)
        else:
            lam_init = 0.8 - 0.6 * math.exp(-0.3 * layer)
            tabs = (_qk_tables(cos, sin, c_q_norm[j], ATTN_SCALE * math.log2(math.e))
                    + _qk_tables(cos, sin, c_k_norm[j], 1.0))
            q, k, vt5 = _odd_in(x2, bsz, seq, mix_norm[layer], c_w_in[j].astype(BF16), tabs,
                                _block_ones(LANES, HEAD_DIM), min(ATTN_TK, seq))
            lam_vecs = jnp.stack([c_lq1[j], c_lk1[j], c_lq2[j], c_lk2[j]])
            o = _attn(q.reshape(bsz, seq, d), k.reshape(bsz, seq, d), vt5, lam_vecs,
                      c_sub_norm[j].reshape(1, LANES), lam_init, ATTN_TQ)
            x2 = _proj_ffn(o.reshape(t, d), x2, c_w_out[j].astype(BF16), ffn_w(layer, 1), tm_proj)
    return x2.reshape(bsz, seq, d)
```

```python
import functools
import math

import jax
import jax.numpy as jnp
from jax import lax
from jax.experimental import pallas as pl
from jax.experimental.pallas import tpu as pltpu

F32 = jnp.float32
BF16 = jnp.bfloat16

HEAD_DIM = 64
LANES = 128
CONV_WIDTH = 31
CONV_HALO = 32
DECAY_RANK = 64
ICL_RANK = 64
GATE_RANK = 128
GN_EPS = 64e-5
ROPE_THETA = 10000.0
ATTN_SCALE = HEAD_DIM ** -0.5
CHUNK = 64
VMEM_LIMIT = 56 * 1024 * 1024


def _const_spec(shape):
    return pl.BlockSpec(shape, lambda *_: (0,) * len(shape), pipeline_mode=pl.Buffered(1))


def _params(*sem):
    return pltpu.CompilerParams(dimension_semantics=sem, vmem_limit_bytes=VMEM_LIMIT)


def _dot(a, b):
    return jnp.dot(a.astype(BF16), b.astype(BF16), preferred_element_type=F32)


def _dot_nt(a, b):
    return lax.dot_general(a.astype(BF16), b.astype(BF16), (((1,), (1,)), ((), ())),
                           preferred_element_type=F32)


def _dot_tn(a, b):
    return lax.dot_general(a.astype(BF16), b.astype(BF16), (((0,), (0,)), ((), ())),
                           preferred_element_type=F32)


def _split2(x):
    hi = x.astype(BF16)
    lo = (x - hi.astype(F32)).astype(BF16)
    return hi, lo


def _split3(x):
    hi = x.astype(BF16)
    r1 = x - hi.astype(F32)
    mid = r1.astype(BF16)
    lo = (r1 - mid.astype(F32)).astype(BF16)
    return hi, mid, lo


def _dot_sel(x, sel):
    return jnp.dot(x.astype(BF16), sel, preferred_element_type=F32)


def _dot_x3(a, b_hi, b_lo):
    a_hi, a_lo = _split2(a)
    return (jnp.dot(a_hi, b_hi, preferred_element_type=F32)
            + jnp.dot(a_lo, b_hi, preferred_element_type=F32)
            + jnp.dot(a_hi, b_lo, preferred_element_type=F32))


def _rms(x, g, eps):
    return x * lax.rsqrt(jnp.mean(x * x, axis=-1, keepdims=True) + eps) * g


def _sigmoid(x):
    return 1.0 / (1.0 + jnp.exp(-x))


FFN_TM = 512


def _swiglu_step(x, g_ref, wg_ref, wu_ref, wd_ref):
    h = _rms(x, g_ref[...], 1e-6).astype(BF16)
    gate = jnp.dot(h, wg_ref[...], preferred_element_type=F32)
    up = jnp.dot(h, wu_ref[...], preferred_element_type=F32)
    act = (gate * _sigmoid(gate) * up).astype(BF16)
    return x + 0.5 * jnp.dot(act, wd_ref[...], preferred_element_type=F32)


def _ffn_body(x_ref, g_ref, wg_ref, wu_ref, wd_ref, o_ref):
    o_ref[...] = _swiglu_step(x_ref[...], g_ref, wg_ref, wu_ref, wd_ref)


def _ffn_specs(d, f):
    return [_const_spec((1, d)), _const_spec((d, f)), _const_spec((d, f)), _const_spec((f, d))]


def _ffn(x2, ffn_w, tm):
    t, d = x2.shape
    row = pl.BlockSpec((tm, d), lambda i: (i, 0))
    return pl.pallas_call(
        _ffn_body,
        out_shape=jax.ShapeDtypeStruct((t, d), F32),
        grid=(t // tm,),
        in_specs=[row] + _ffn_specs(d, ffn_w[1].shape[1]),
        out_specs=row,
        compiler_params=_params("parallel"),
        name="ffn",
    )(x2, *ffn_w)


def _rwkv_prep_tile(aw, za, mu_ref, w0_ref, w2h_ref, w2l_ref, a0_ref, a2_ref, g2_ref, kk_ref_w, ka_ref,
                    rk_ref, sel_ref, r_out, lw_out, k_out, v_out, kk_out, a_out, g_out, bonus_out, prev_ref):
    tm = za.shape[0]
    j = pl.program_id(1)

    @pl.when(j == 0)
    def _():
        prev_ref[...] = jnp.zeros_like(prev_ref)

    row = lax.broadcasted_iota(jnp.int32, za.shape, 0)
    shifted = jnp.where(row == 0, prev_ref[0:1, :], pltpu.roll(za, 1, 0))
    prev_ref[0:1, :] = za[tm - 1:tm, :]
    xr = za + (shifted - za) * mu_ref[...]

    r = xr[:, 0:aw]
    k = xr[:, aw:2 * aw]
    v = xr[:, 2 * aw:3 * aw]
    lr = xr[:, 3 * aw:3 * aw + DECAY_RANK + ICL_RANK]
    gd = xr[:, 3 * aw + DECAY_RANK + ICL_RANK:]

    pre = w0_ref[...] + _dot_x3(jnp.tanh(lr), w2h_ref[...], w2l_ref[...])
    softplus_neg = jnp.maximum(-pre, 0.0) + jnp.log(1.0 + jnp.exp(-jnp.abs(pre)))
    w_log = -softplus_neg - 0.5
    lw_out[...] = -jnp.exp(w_log)

    a = _sigmoid(a0_ref[...] + jnp.dot(lr.astype(BF16), a2_ref[...], preferred_element_type=F32))
    g_out[...] = jnp.dot(_sigmoid(gd).astype(BF16), g2_ref[...], preferred_element_type=F32)

    sel = sel_ref[...]
    kk = k * kk_ref_w[...]
    kk = kk / jnp.maximum(jnp.sqrt(_dot_sel(kk * kk, sel)), 1e-12)
    kmod = k * (1.0 + (a - 1.0) * ka_ref[...])
    bonus_out[...] = _dot_sel(r * kmod * rk_ref[...], sel) * v
    r_out[...] = r
    k_out[...] = kmod
    v_out[...] = v
    kk_out[...] = kk
    a_out[...] = a


def _conv_tile(bw, zb, bias_ref, dw_ref, dwb_ref, lnw_ref, lnb_ref, y_ref, buf_ref, shift_ref):
    tm = zb.shape[0]
    j = pl.program_id(1)
    u = zb + bias_ref[...]
    gl = u[:, :bw] * _sigmoid(u[:, bw:])

    @pl.when(j == 0)
    def _():
        buf_ref[0:CONV_HALO, :] = jnp.zeros((CONV_HALO, bw), F32)

    @pl.when(j > 0)
    def _():
        buf_ref[0:CONV_HALO, :] = buf_ref[tm:tm + CONV_HALO, :]

    buf_ref[CONV_HALO:, :] = gl
    base = CONV_HALO - (CONV_WIDTH - 1)
    acc = jnp.zeros((tm, bw), F32)
    for shift in range(8):
        taps = [tap for tap in range(CONV_WIDTH) if (base + tap) % 8 == shift]
        if not taps:
            continue
        span = (base + taps[-1]) // 8 * 8
        if shift:
            shift_ref[shift - 1, 0:tm + span, :] = buf_ref[pl.ds(shift, tm + span), :]
        for tap in taps:
            off = (base + tap) // 8 * 8
            rows = shift_ref[shift - 1, off:off + tm, :] if shift else buf_ref[off:off + tm, :]
            acc = acc + dw_ref[tap:tap + 1, :] * rows
    c = acc + dwb_ref[...]
    mu = jnp.mean(c, axis=-1, keepdims=True)
    var = jnp.mean(jnp.square(c - mu), axis=-1, keepdims=True)
    y = (c - mu) * lax.rsqrt(var + 1e-5) * lnw_ref[...] + lnb_ref[...]
    y_ref[...] = y * _sigmoid(y)


N_PREP_IN = 11
N_CONV_IN = 5


def _even_front_body(aw, a_proj, bw, x_ref, g_ref, w_ref, *refs):
    prep_in, refs = refs[:N_PREP_IN], refs[N_PREP_IN:]
    conv_in, refs = refs[:N_CONV_IN], refs[N_CONV_IN:]
    prep_out, yb_out, (prev_ref, buf_ref, shift_ref) = refs[:8], refs[8], refs[9:]
    h = _rms(x_ref[...], g_ref[...], 1e-6).astype(BF16)
    z = jnp.dot(h, w_ref[...], preferred_element_type=F32)
    _rwkv_prep_tile(aw, z[:, :a_proj], *prep_in, *prep_out, prev_ref)
    _conv_tile(bw, z[:, a_proj:], *conv_in, yb_out, buf_ref, shift_ref)


def _even_front(x2, bsz, seq, g, w, a_proj, prep_params, conv_params, tm):
    t, d = x2.shape
    n = w.shape[1]
    aw = prep_params[1].shape[1]
    bw = (n - a_proj) // 2
    nt = seq // tm
    row = lambda b, j: (b * nt + j, 0)
    whole = lambda arr: _const_spec(arr.shape)
    assert len(prep_params) == N_PREP_IN and len(conv_params) == N_CONV_IN
    return pl.pallas_call(
        functools.partial(_even_front_body, aw, a_proj, bw),
        out_shape=(jax.ShapeDtypeStruct((t, aw), F32),) * 8 + (jax.ShapeDtypeStruct((t, bw), F32),),
        grid=(bsz, nt),
        in_specs=[pl.BlockSpec((tm, d), row), _const_spec((1, d)), _const_spec((d, n))]
                 + [whole(p) for p in prep_params] + [whole(p) for p in conv_params],
        out_specs=(pl.BlockSpec((tm, aw), row),) * 8 + (pl.BlockSpec((tm, bw), row),),
        scratch_shapes=[pltpu.VMEM((8, a_proj), F32), pltpu.VMEM((tm + CONV_HALO, bw), F32),
                        pltpu.VMEM((7, tm + CONV_HALO - 8, bw), F32)],
        compiler_params=_params("parallel", "arbitrary"),
        name="even_front",
    )(x2, g.reshape(1, d), w, *prep_params, *conv_params)


def _wkv_body(r_ref, lw_ref, k_ref, v_ref, kk_ref, a_ref, y_ref, state_ref):
    nseq, ch, width = r_ref.shape
    n_pairs = width // LANES
    c = pl.program_id(1)

    @pl.when(c == 0)
    def _():
        state_ref[...] = jnp.zeros_like(state_ref)

    ti = lax.broadcasted_iota(jnp.int32, (ch, ch), 0)
    tj = lax.broadcasted_iota(jnp.int32, (ch, ch), 1)
    tri = (ti >= tj).astype(BF16)
    lane = lax.broadcasted_iota(jnp.int32, (ch, LANES), 1)
    head0 = lane < HEAD_DIM

    def stack(x):
        zero = jnp.zeros_like(x)
        return jnp.concatenate([jnp.where(head0, x, zero), jnp.where(head0, zero, x)], axis=0)

    n = 2 * ch
    ri = lax.broadcasted_iota(jnp.int32, (n, n), 0) & (ch - 1)
    ci = lax.broadcasted_iota(jnp.int32, (n, n), 1) & (ch - 1)
    strict = ri > ci
    incl = ri >= ci
    eye = (lax.broadcasted_iota(jnp.int32, (n, n), 0) == lax.broadcasted_iota(jnp.int32, (n, n), 1)).astype(F32)

    rd, ad, bd, kd, bhd, khd, vd, g_last, where = [], [], [], [], [], [], [], [], []
    for s in range(nseq):
        lw = lw_ref[s]
        lg = sum(jnp.dot(tri, part, preferred_element_type=F32) for part in _split3(lw))
        lg_last = lg[ch - 1:ch, :]
        e_pos = jnp.exp(lg)
        e_neg = jnp.exp(-lg)
        e_prev = jnp.exp(lg - lw)
        e_tail = jnp.exp(lg_last - lg)
        kk = kk_ref[s]
        kka = kk * a_ref[s]
        kmod = k_ref[s]
        r_t = (r_ref[s] * e_pos).astype(BF16)
        a_t = (-kk * e_prev).astype(BF16)
        b_t = (kka * e_neg).astype(BF16)
        k_t = (kmod * e_neg).astype(BF16)
        b_h = (kka * e_tail).astype(BF16)
        k_h = (kmod * e_tail).astype(BF16)
        v_b = v_ref[s].astype(BF16)
        g_s = jnp.exp(lg_last)
        for p in range(n_pairs):
            sl = slice(p * LANES, (p + 1) * LANES)
            rd.append(stack(r_t[:, sl]))
            ad.append(stack(a_t[:, sl]))
            bd.append(stack(b_t[:, sl]))
            kd.append(stack(k_t[:, sl]))
            bhd.append(stack(b_h[:, sl]))
            khd.append(stack(k_h[:, sl]))
            vd.append(stack(v_b[:, sl]))
            g_last.append(g_s[:, sl])
            where.append((s, p, sl))
    units = range(len(where))

    ob = [_dot_nt(jnp.concatenate([ad[u], rd[u]], axis=0), bd[u]) for u in units]
    ok = [_dot_nt(jnp.concatenate([ad[u], rd[u]], axis=0), kd[u]) for u in units]
    a_ab = [jnp.where(strict, ob[u][:n], 0.0) for u in units]
    a_rb = [jnp.where(incl, ob[u][n:], 0.0).astype(BF16) for u in units]
    a_ak = [jnp.where(strict, ok[u][:n], 0.0) for u in units]
    a_rk = [jnp.where(incl, ok[u][n:], 0.0).astype(BF16) for u in units]
    akv = [_dot(a_ak[u], vd[u]) for u in units]

    tinv = [eye + a_ab[u] for u in units]
    pw = [_dot(a_ab[u], a_ab[u]) for u in units]
    for _ in range(int(math.log2(ch)) - 2):
        both = [_dot(jnp.concatenate([tinv[u], pw[u]], axis=0), pw[u]) for u in units]
        tinv = [tinv[u] + both[u][:n] for u in units]
        pw = [both[u][n:] for u in units]
    tinv = [tinv[u] + _dot(tinv[u], pw[u]) for u in units]

    w12 = [_dot(tinv[u], jnp.concatenate([ad[u], akv[u].astype(BF16)], axis=1)) for u in units]
    s_old = [state_ref[s, p] for s, p, _ in where]
    from_state = [_dot_nt(jnp.concatenate([w12[u][:, :LANES].astype(BF16), rd[u]], axis=0), s_old[u]) for u in units]
    uv = [jnp.concatenate([(from_state[u][:n] + w12[u][:, LANES:]).astype(BF16), vd[u]], axis=0) for u in units]
    y = [from_state[u][n:] + _dot(jnp.concatenate([a_rb[u], a_rk[u]], axis=1), uv[u]) for u in units]
    for u, (s, p, sl) in enumerate(where):
        y_ref[s, :, sl] = y[u][:ch] + y[u][ch:]
        state_ref[s, p] = s_old[u] * g_last[u] + _dot_tn(uv[u], jnp.concatenate([bhd[u], khd[u]], axis=0))


WKV_SEQS = 2


def _wkv(r, lw, k, v, kk, a, bsz, seq):
    width = r.shape[-1]
    nc = seq // CHUNK
    nseq = WKV_SEQS if bsz % WKV_SEQS == 0 else 1
    spec = pl.BlockSpec((nseq, CHUNK, width), lambda b, c: (b, c, 0))
    args = [t.reshape(bsz, seq, width) for t in (r, lw, k, v, kk, a)]
    y = pl.pallas_call(
        _wkv_body,
        out_shape=jax.ShapeDtypeStruct((bsz, seq, width), F32),
        grid=(bsz // nseq, nc),
        in_specs=[spec] * 6,
        out_specs=spec,
        scratch_shapes=[pltpu.VMEM((nseq, width // LANES, LANES, LANES), F32)],
        compiler_params=_params("parallel", "arbitrary"),
        name="wkv7_chunk",
    )(*args)
    return y.reshape(bsz * seq, width)


def _even_out_ffn_body(aw, y_ref, bonus_ref, gate_ref, yb_ref, x_ref, lnw_ref, lnb_ref, sel_ref, w_ref,
                       g_ref, wg_ref, wu_ref, wd_ref, o_ref):
    sel = sel_ref[...]
    y = y_ref[...]
    inv_n = 1.0 / HEAD_DIM
    d = y - _dot_sel(y, sel) * inv_n
    var = _dot_sel(d * d, sel) * inv_n
    yn = d * lax.rsqrt(var + GN_EPS) * lnw_ref[...] + lnb_ref[...]
    ya = ((yn + bonus_ref[...]) * gate_ref[...]).astype(BF16)
    x = (x_ref[...]
         + jnp.dot(ya, w_ref[0:aw, :], preferred_element_type=F32)
         + jnp.dot(yb_ref[...].astype(BF16), w_ref[aw:, :], preferred_element_type=F32))
    o_ref[...] = _swiglu_step(x, g_ref, wg_ref, wu_ref, wd_ref)


def _even_out_ffn(y, bonus, gate, yb, x2, lnw, lnb, sel, w, ffn_w, tm):
    t, d = x2.shape
    aw = y.shape[1]
    bw = yb.shape[1]
    row = lambda i: (i, 0)
    return pl.pallas_call(
        functools.partial(_even_out_ffn_body, aw),
        out_shape=jax.ShapeDtypeStruct((t, d), F32),
        grid=(t // tm,),
        in_specs=[pl.BlockSpec((tm, aw), row), pl.BlockSpec((tm, aw), row), pl.BlockSpec((tm, aw), row),
                  pl.BlockSpec((tm, bw), row), pl.BlockSpec((tm, d), row), _const_spec((1, aw)),
                  _const_spec((1, aw)), _const_spec((aw, aw)), _const_spec((d, d))]
                 + _ffn_specs(d, ffn_w[1].shape[1]),
        out_specs=pl.BlockSpec((tm, d), row),
        compiler_params=_params("parallel"),
        name="even_out_ffn",
    )(y, bonus, gate, yb, x2, lnw, lnb, sel, w, *ffn_w)


ODD_IN_SPLIT = 2


def _odd_in_body(d, x_ref, g_ref, w_ref, qc_ref, qs_ref, kc_ref, ks_ref, sel_ref, q_ref, k_ref, vt_ref):
    tm = x_ref.shape[0]
    sub = tm // ODD_IN_SPLIT
    sel = sel_ref[...]
    lane = lax.broadcasted_iota(jnp.int32, (sub, LANES), 1)
    first_half = (lane & (HEAD_DIM - 1)) < HEAD_DIM // 2
    zs = []
    for part in range(ODD_IN_SPLIT):
        rows = slice(part * sub, (part + 1) * sub)
        h = _rms(x_ref[rows, :], g_ref[...], 1e-6).astype(BF16)
        zs.append(jnp.dot(h, w_ref[...], preferred_element_type=F32))
    for part, z in enumerate(zs):
        rows = slice(part * sub, (part + 1) * sub)
        for blk in range(d // LANES):
            sl = slice(blk * LANES, (blk + 1) * LANES)
            for off, c_ref, s_ref, out_ref in ((0, qc_ref, qs_ref, q_ref), (d, kc_ref, ks_ref, k_ref)):
                xq = z[:, off + blk * LANES:off + (blk + 1) * LANES]
                ss = jnp.dot((xq * xq).astype(BF16), sel, preferred_element_type=F32)
                swapped = jnp.where(first_half, pltpu.roll(xq, LANES - HEAD_DIM // 2, 1),
                                    pltpu.roll(xq, HEAD_DIM // 2, 1))
                out = lax.rsqrt(ss + HEAD_DIM * 1e-6) * (xq * c_ref[rows, :] + swapped * s_ref[rows, :])
                out_ref[rows, sl] = out.astype(BF16)
        for hd in range(d // LANES):
            vt_ref[hd, :, rows] = z[:, 2 * d + hd * LANES:2 * d + (hd + 1) * LANES].T.astype(BF16)


def _odd_in(x2, bsz, seq, g, w, tabs, sel, tm):
    t, d = x2.shape
    nt = seq // tm
    heads = d // LANES
    row = lambda b, j: (b * nt + j, 0)
    pos = lambda b, j: (j, 0)
    out = jax.ShapeDtypeStruct((t, d), BF16)
    return pl.pallas_call(
        functools.partial(_odd_in_body, d),
        out_shape=(out, out, jax.ShapeDtypeStruct((bsz, heads, nt, LANES, tm), BF16)),
        grid=(bsz, nt),
        in_specs=[pl.BlockSpec((tm, d), row), _const_spec((1, d)), _const_spec((d, 3 * d))]
                 + [pl.BlockSpec((tm, LANES), pos)] * 4 + [_const_spec((LANES, LANES))],
        out_specs=(pl.BlockSpec((tm, d), row), pl.BlockSpec((tm, d), row),
                   pl.BlockSpec((None, heads, None, LANES, tm), lambda b, j: (b, 0, j, 0, 0))),
        compiler_params=_params("parallel", "parallel"),
        name="odd_in",
    )(x2, g.reshape(1, d), w, *tabs, sel)


ATTN_TQ = 512
ATTN_TK = 512
ATTN_GROUP = 2
SUM_ROWS = 16


def _attn_body(lam_init, q_ref, k_ref, vt_ref, lam_ref, sn_ref, o_ref, qd_buf, s_buf, p_buf, al_buf, m_ref, acc_ref):
    seq = q_ref.shape[0]
    tk = vt_ref.shape[-1]
    tq = qd_buf.shape[1] // 2
    per_q = tq // tk
    ones = jnp.ones((SUM_ROWS, tk), BF16)
    lv = lam_ref[...]
    lam = (jnp.exp(jnp.sum(lv[0:1] * lv[1:2], axis=-1, keepdims=True))
           - jnp.exp(jnp.sum(lv[2:3] * lv[3:4], axis=-1, keepdims=True)) + lam_init)
    lane = lax.broadcasted_iota(jnp.int32, (tq, LANES), 1)
    comp0 = lane < HEAD_DIM

    def scores(i, j, slot):
        if j == 0:
            q = q_ref[i * tq:(i + 1) * tq, :]
            zero = jnp.zeros_like(q)
            qd_buf[i & 1] = jnp.concatenate([jnp.where(comp0, q, zero), jnp.where(comp0, zero, q)], axis=0)
            m_ref[i & 1] = jnp.full(m_ref.shape[1:], -1e30, F32)
            acc_ref[i & 1] = jnp.zeros(acc_ref.shape[1:], F32)
        kj = k_ref[j * tk:(j + 1) * tk, :]
        s_buf[slot] = lax.dot_general(kj, qd_buf[i & 1], (((1,), (1,)), ((), ())),
                                      preferred_element_type=F32)

    def softmax(i, j, slot):
        s = s_buf[slot]
        first_diag = per_q * i
        if j >= first_diag:
            ki = lax.broadcasted_iota(jnp.int32, s.shape, 0) + (j - first_diag) * tk
            qi = lax.broadcasted_iota(jnp.int32, s.shape, 1) & (tq - 1)
            s = jnp.where(ki <= qi, s, -jnp.inf)
        m_old = m_ref[i & 1]
        m_new = jnp.maximum(m_old, jnp.max(s, axis=0, keepdims=True))
        m_ref[i & 1] = m_new
        al_buf[slot] = jnp.exp2(m_old - m_new)
        p_buf[slot] = jnp.exp2(s - m_new).astype(BF16)

    def values(i, j, slot):
        vj = jnp.concatenate([vt_ref[j], ones], axis=0)
        acc_ref[i & 1] = al_buf[slot] * acc_ref[i & 1] + jnp.dot(vj, p_buf[slot], preferred_element_type=F32)
        if j == per_q * (i + 1) - 1:
            acc = acc_ref[i & 1]
            o = acc[:LANES] / acc[LANES:LANES + 1]
            out = (o[:, :tq] - lam * o[:, tq:]).T
            out = _rms(out, sn_ref[...], 1e-5) * (1.0 - lam_init)
            o_ref[i * tq:(i + 1) * tq, :] = out.astype(BF16)

    blocks = [(i, j) for i in range(seq // tq) for j in range(per_q * (i + 1))]
    ring = s_buf.shape[0]
    group = ring // 2
    n_groups = -(-len(blocks) // group)
    for u in range(n_groups + 2):
        for stage, lag in ((values, 2), (softmax, 1), (scores, 0)):
            for g in range((u - lag) * group, (u - lag + 1) * group):
                if 0 <= g < len(blocks):
                    stage(*blocks[g], g % ring)


def _attn(q3, k3, vt5, lam_vecs, sub_norm, lam_init, tq):
    bsz, seq, d = q3.shape
    heads = d // LANES
    nkv, _, tk = vt5.shape[2:]
    assert tq % tk == 0 and seq % tq == 0
    ring = 2 * ATTN_GROUP
    head_rows = pl.BlockSpec((None, seq, LANES), lambda b, h: (b, 0, h))
    return pl.pallas_call(
        functools.partial(_attn_body, lam_init),
        out_shape=jax.ShapeDtypeStruct((bsz, seq, d), BF16),
        grid=(bsz, heads),
        in_specs=[head_rows, head_rows,
                  pl.BlockSpec((None, None, nkv, LANES, tk), lambda b, h: (b, h, 0, 0, 0)),
                  _const_spec((4, HEAD_DIM)), _const_spec((1, LANES))],
        out_specs=head_rows,
        scratch_shapes=[pltpu.VMEM((2, 2 * tq, LANES), BF16), pltpu.VMEM((ring, tk, 2 * tq), F32),
                        pltpu.VMEM((ring, tk, 2 * tq), BF16), pltpu.VMEM((ring, 1, 2 * tq), F32),
                        pltpu.VMEM((2, 1, 2 * tq), F32), pltpu.VMEM((2, LANES + SUM_ROWS, 2 * tq), F32)],
        compiler_params=_params("parallel", "parallel"),
        name="diff_attn",
    )(q3, k3, vt5, lam_vecs, sub_norm)


def _proj_ffn_body(o_ref, x_ref, w_ref, g_ref, wg_ref, wu_ref, wd_ref, out_ref):
    x = x_ref[...] + jnp.dot(o_ref[...], w_ref[...], preferred_element_type=F32)
    out_ref[...] = _swiglu_step(x, g_ref, wg_ref, wu_ref, wd_ref)


def _proj_ffn(o2, x2, w, ffn_w, tm):
    t, d = x2.shape
    row = pl.BlockSpec((tm, d), lambda i: (i, 0))
    return pl.pallas_call(
        _proj_ffn_body,
        out_shape=jax.ShapeDtypeStruct((t, d), F32),
        grid=(t // tm,),
        in_specs=[row, row, _const_spec((d, d))] + _ffn_specs(d, ffn_w[1].shape[1]),
        out_specs=row,
        compiler_params=_params("parallel"),
        name="proj_ffn",
    )(o2, x2, w, *ffn_w)


def _block_ones(n, blk):
    idx = jnp.arange(n) // blk
    return (idx[:, None] == idx[None, :]).astype(BF16)


def _rope_tables(seq):
    inv = 1.0 / (ROPE_THETA ** (jnp.arange(0, HEAD_DIM, 2, dtype=F32) / HEAD_DIM))
    ang = jnp.arange(seq, dtype=F32)[:, None] * inv[None, :]
    cos, sin = jnp.cos(ang), jnp.sin(ang)
    reps = LANES // HEAD_DIM
    return jnp.tile(jnp.concatenate([cos, cos], -1), (1, reps)), jnp.tile(jnp.concatenate([-sin, sin], -1), (1, reps))


def _qk_tables(cos, sin, norm, scale):
    c = scale * math.sqrt(HEAD_DIM)
    reps = LANES // HEAD_DIM
    gain = jnp.tile(norm, reps) * c
    gain_swapped = jnp.tile(jnp.roll(norm, HEAD_DIM // 2), reps) * c
    return cos * gain[None, :], sin * gain_swapped[None, :]


def _row_tile(seq):
    return min(256, seq)


def kernel(x, ffn_norm, ffn_w_gate, ffn_w_up, ffn_w_down, mix_norm, a_w_in, a_mu, a_w0, a_w2, a_a0, a_a2, a_g2, a_k_k, a_k_a, a_r_k, a_ln_w, a_ln_b, b_glu_bias, b_dw, b_dw_bias, b_ln_w, b_ln_b, e_w_out, c_w_in, c_q_norm, c_k_norm, c_lq1, c_lk1, c_lq2, c_lk2, c_sub_norm, c_w_out):
    bsz, seq, d = x.shape
    depth = ffn_norm.shape[0]
    t = bsz * seq
    tm = _row_tile(seq)
    tm_proj = min(FFN_TM, seq)
    x2 = x.reshape(t, d)
    wg, wu, wd = ffn_w_gate.astype(BF16), ffn_w_up.astype(BF16), ffn_w_down.astype(BF16)
    cos, sin = _rope_tables(seq)

    def ffn_w(layer, which):
        return (ffn_norm[layer, which].reshape(1, d), wg[layer, which], wu[layer, which], wd[layer, which])

    for layer in range(depth):
        x2 = _ffn(x2, ffn_w(layer, 0), tm_proj)
        j = layer // 2
        if layer % 2 == 0:
            aw = a_w0.shape[1]
            a_proj = a_mu.shape[1]
            row = lambda v: v.reshape(1, -1)
            sel = _block_ones(aw, HEAD_DIM)
            w2p = jnp.pad(a_w2[j], ((0, ICL_RANK), (0, 0)))
            w2h = w2p.astype(BF16)
            w2l = (w2p - w2h.astype(F32)).astype(BF16)
            a2p = jnp.pad(a_a2[j], ((DECAY_RANK, 0), (0, 0))).astype(BF16)
            prep_params = (row(a_mu[j]), row(a_w0[j]), w2h, w2l, row(a_a0[j]), a2p, a_g2[j].astype(BF16),
                           row(a_k_k[j]), row(a_k_a[j]), row(a_r_k[j]), sel)
            conv_params = (row(b_glu_bias[j]), jnp.pad(b_dw[j], ((0, CONV_HALO - CONV_WIDTH), (0, 0))),
                           row(b_dw_bias[j]), row(b_ln_w[j]), row(b_ln_b[j]))
            r, lw, k, v, kk, a, g, bonus, yb = _even_front(
                x2, bsz, seq, mix_norm[layer], a_w_in[j].astype(BF16), a_proj, prep_params, conv_params, tm)
            y = _wkv(r, lw, k, v, kk, a, bsz, seq)
            x2 = _even_out_ffn(y, bonus, g, yb, x2, row(a_ln_w[j]), row(a_ln_b[j]), sel, e_w_out[j].astype(BF16),
                               ffn_w(layer, 1), tm_proj)
        else:
            lam_init = 0.8 - 0.6 * math.exp(-0.3 * layer)
            tabs = (_qk_tables(cos, sin, c_q_norm[j], ATTN_SCALE * math.log2(math.e))
                    + _qk_tables(cos, sin, c_k_norm[j], 1.0))
            q, k, vt5 = _odd_in(x2, bsz, seq, mix_norm[layer], c_w_in[j].astype(BF16), tabs,
                                _block_ones(LANES, HEAD_DIM), min(ATTN_TK, seq))
            lam_vecs = jnp.stack([c_lq1[j], c_lk1[j], c_lq2[j], c_lk2[j]])
            o = _attn(q.reshape(bsz, seq, d), k.reshape(bsz, seq, d), vt5, lam_vecs,
                      c_sub_norm[j].reshape(1, LANES), lam_init, ATTN_TQ)
            x2 = _proj_ffn(o.reshape(t, d), x2, c_w_out[j].astype(BF16), ffn_w(layer, 1), tm_proj)
    return x2.reshape(bsz, seq, d)
```

```python
import functools
import math

import jax
import jax.numpy as jnp
from jax import lax
from jax.experimental import pallas as pl
from jax.experimental.pallas import tpu as pltpu

F32 = jnp.float32
BF16 = jnp.bfloat16

HEAD_DIM = 64
LANES = 128
CONV_WIDTH = 31
CONV_HALO = 32
DECAY_RANK = 64
ICL_RANK = 64
GATE_RANK = 128
GN_EPS = 64e-5
ROPE_THETA = 10000.0
ATTN_SCALE = HEAD_DIM ** -0.5
CHUNK = 64
VMEM_LIMIT = 56 * 1024 * 1024


def _const_spec(shape):
    return pl.BlockSpec(shape, lambda *_: (0,) * len(shape), pipeline_mode=pl.Buffered(1))


def _params(*sem):
    return pltpu.CompilerParams(dimension_semantics=sem, vmem_limit_bytes=VMEM_LIMIT)


def _dot(a, b):
    return jnp.dot(a.astype(BF16), b.astype(BF16), preferred_element_type=F32)


def _dot_nt(a, b):
    return lax.dot_general(a.astype(BF16), b.astype(BF16), (((1,), (1,)), ((), ())),
                           preferred_element_type=F32)


def _dot_tn(a, b):
    return lax.dot_general(a.astype(BF16), b.astype(BF16), (((0,), (0,)), ((), ())),
                           preferred_element_type=F32)


def _split2(x):
    hi = x.astype(BF16)
    lo = (x - hi.astype(F32)).astype(BF16)
    return hi, lo


def _split3(x):
    hi = x.astype(BF16)
    r1 = x - hi.astype(F32)
    mid = r1.astype(BF16)
    lo = (r1 - mid.astype(F32)).astype(BF16)
    return hi, mid, lo


def _dot_sel(x, sel):
    return jnp.dot(x.astype(BF16), sel, preferred_element_type=F32)


def _dot_x3(a, b_hi, b_lo):
    a_hi, a_lo = _split2(a)
    return (jnp.dot(a_hi, b_hi, preferred_element_type=F32)
            + jnp.dot(a_lo, b_hi, preferred_element_type=F32)
            + jnp.dot(a_hi, b_lo, preferred_element_type=F32))


def _rms(x, g, eps):
    return x * lax.rsqrt(jnp.mean(x * x, axis=-1, keepdims=True) + eps) * g


def _sigmoid(x):
    return 1.0 / (1.0 + jnp.exp(-x))


FFN_TM = 512


def _swiglu_step(x, g_ref, wg_ref, wu_ref, wd_ref):
    h = _rms(x, g_ref[...], 1e-6).astype(BF16)
    gate = jnp.dot(h, wg_ref[...], preferred_element_type=F32)
    up = jnp.dot(h, wu_ref[...], preferred_element_type=F32)
    act = (gate * _sigmoid(gate) * up).astype(BF16)
    return x + 0.5 * jnp.dot(act, wd_ref[...], preferred_element_type=F32)


def _ffn_body(x_ref, g_ref, wg_ref, wu_ref, wd_ref, o_ref):
    o_ref[...] = _swiglu_step(x_ref[...], g_ref, wg_ref, wu_ref, wd_ref)


def _ffn_specs(d, f):
    return [_const_spec((1, d)), _const_spec((d, f)), _const_spec((d, f)), _const_spec((f, d))]


def _ffn(x2, ffn_w, tm):
    t, d = x2.shape
    row = pl.BlockSpec((tm, d), lambda i: (i, 0))
    return pl.pallas_call(
        _ffn_body,
        out_shape=jax.ShapeDtypeStruct((t, d), F32),
        grid=(t // tm,),
        in_specs=[row] + _ffn_specs(d, ffn_w[1].shape[1]),
        out_specs=row,
        compiler_params=_params("parallel"),
        name="ffn",
    )(x2, *ffn_w)


def _rwkv_prep_tile(aw, za, mu_ref, w0_ref, w2h_ref, w2l_ref, a0_ref, a2_ref, g2_ref, kk_ref_w, ka_ref,
                    rk_ref, sel_ref, r_out, lw_out, k_out, v_out, kk_out, a_out, g_out, bonus_out, prev_ref):
    tm = za.shape[0]
    row = lax.broadcasted_iota(jnp.int32, za.shape, 0)
    shifted = jnp.where(row == 0, prev_ref[0:1, :], pltpu.roll(za, 1, 0))
    prev_ref[0:1, :] = za[tm - 1:tm, :]
    xr = za + (shifted - za) * mu_ref[...]

    r = xr[:, 0:aw]
    k = xr[:, aw:2 * aw]
    v = xr[:, 2 * aw:3 * aw]
    lr = xr[:, 3 * aw:3 * aw + DECAY_RANK + ICL_RANK]
    gd = xr[:, 3 * aw + DECAY_RANK + ICL_RANK:]

    pre = w0_ref[...] + _dot_x3(jnp.tanh(lr), w2h_ref[...], w2l_ref[...])
    softplus_neg = jnp.maximum(-pre, 0.0) + jnp.log(1.0 + jnp.exp(-jnp.abs(pre)))
    w_log = -softplus_neg - 0.5
    lw_out[...] = -jnp.exp(w_log)

    a = _sigmoid(a0_ref[...] + jnp.dot(lr.astype(BF16), a2_ref[...], preferred_element_type=F32))
    g_out[...] = jnp.dot(_sigmoid(gd).astype(BF16), g2_ref[...], preferred_element_type=F32)

    sel = sel_ref[...]
    kk = k * kk_ref_w[...]
    kk = kk / jnp.maximum(jnp.sqrt(_dot_sel(kk * kk, sel)), 1e-12)
    kmod = k * (1.0 + (a - 1.0) * ka_ref[...])
    bonus_out[...] = _dot_sel(r * kmod * rk_ref[...], sel) * v
    r_out[...] = r
    k_out[...] = kmod
    v_out[...] = v
    kk_out[...] = kk
    a_out[...] = a


def _conv_tile(bw, zb, bias_ref, dw_ref, dwb_ref, lnw_ref, lnb_ref, y_ref, buf_ref, shift_ref):
    tm = zb.shape[0]
    u = zb + bias_ref[...]
    gl = u[:, :bw] * _sigmoid(u[:, bw:])
    buf_ref[0:CONV_HALO, :] = buf_ref[tm:tm + CONV_HALO, :]
    buf_ref[CONV_HALO:, :] = gl
    base = CONV_HALO - (CONV_WIDTH - 1)
    acc = jnp.zeros((tm, bw), F32)
    for shift in range(8):
        taps = [tap for tap in range(CONV_WIDTH) if (base + tap) % 8 == shift]
        if not taps:
            continue
        span = (base + taps[-1]) // 8 * 8
        if shift:
            shift_ref[shift - 1, 0:tm + span, :] = buf_ref[pl.ds(shift, tm + span), :]
        for tap in taps:
            off = (base + tap) // 8 * 8
            rows = shift_ref[shift - 1, off:off + tm, :] if shift else buf_ref[off:off + tm, :]
            acc = acc + dw_ref[tap:tap + 1, :] * rows
    c = acc + dwb_ref[...]
    mu = jnp.mean(c, axis=-1, keepdims=True)
    var = jnp.mean(jnp.square(c - mu), axis=-1, keepdims=True)
    y = (c - mu) * lax.rsqrt(var + 1e-5) * lnw_ref[...] + lnb_ref[...]
    y_ref[...] = y * _sigmoid(y)


N_PREP_IN = 11
N_CONV_IN = 5


def _even_front_body(aw, a_proj, bw, x_ref, g_ref, w_ref, *refs):
    prep_in, refs = refs[:N_PREP_IN], refs[N_PREP_IN:]
    conv_in, refs = refs[:N_CONV_IN], refs[N_CONV_IN:]
    prep_out, yb_out, (prev_ref, buf_ref, shift_ref) = refs[:8], refs[8], refs[9:]
    tm = x_ref.shape[0]

    @pl.when(pl.program_id(1) == 0)
    def _():
        prev_ref[...] = jnp.zeros_like(prev_ref)
        buf_ref[tm:tm + CONV_HALO, :] = jnp.zeros((CONV_HALO, bw), F32)

    h = _rms(x_ref[...], g_ref[...], 1e-6).astype(BF16)
    zb = jnp.dot(h, w_ref[:, :2 * bw], preferred_element_type=F32)
    za = jnp.dot(h, w_ref[:, 2 * bw:], preferred_element_type=F32)
    _conv_tile(bw, zb, *conv_in, yb_out, buf_ref, shift_ref)
    _rwkv_prep_tile(aw, za, *prep_in, *prep_out, prev_ref)


def _even_front(x2, bsz, seq, g, w, a_proj, prep_params, conv_params, tm):
    t, d = x2.shape
    n = w.shape[1]
    aw = prep_params[1].shape[1]
    bw = (n - a_proj) // 2
    nt = seq // tm
    row = lambda b, j: (b * nt + j, 0)
    whole = lambda arr: _const_spec(arr.shape)
    assert len(prep_params) == N_PREP_IN and len(conv_params) == N_CONV_IN
    return pl.pallas_call(
        functools.partial(_even_front_body, aw, a_proj, bw),
        out_shape=(jax.ShapeDtypeStruct((t, aw), F32),) * 8 + (jax.ShapeDtypeStruct((t, bw), F32),),
        grid=(bsz, nt),
        in_specs=[pl.BlockSpec((tm, d), row), _const_spec((1, d)), _const_spec((d, n))]
                 + [whole(p) for p in prep_params] + [whole(p) for p in conv_params],
        out_specs=(pl.BlockSpec((tm, aw), row),) * 8 + (pl.BlockSpec((tm, bw), row),),
        scratch_shapes=[pltpu.VMEM((8, a_proj), F32), pltpu.VMEM((tm + CONV_HALO, bw), F32),
                        pltpu.VMEM((7, tm + CONV_HALO - 8, bw), F32)],
        compiler_params=_params("parallel", "arbitrary"),
        name="even_front",
    )(x2, g.reshape(1, d), w, *prep_params, *conv_params)


def _wkv_body(r_ref, lw_ref, k_ref, v_ref, kk_ref, a_ref, y_ref, state_ref):
    nseq, ch, width = r_ref.shape
    n_pairs = width // LANES
    c = pl.program_id(1)

    @pl.when(c == 0)
    def _():
        state_ref[...] = jnp.zeros_like(state_ref)

    ti = lax.broadcasted_iota(jnp.int32, (ch, ch), 0)
    tj = lax.broadcasted_iota(jnp.int32, (ch, ch), 1)
    tri = (ti >= tj).astype(BF16)
    lane = lax.broadcasted_iota(jnp.int32, (ch, LANES), 1)
    head0 = lane < HEAD_DIM

    def stack(x):
        zero = jnp.zeros_like(x)
        return jnp.concatenate([jnp.where(head0, x, zero), jnp.where(head0, zero, x)], axis=0)

    n = 2 * ch
    ri = lax.broadcasted_iota(jnp.int32, (n, n), 0) & (ch - 1)
    ci = lax.broadcasted_iota(jnp.int32, (n, n), 1) & (ch - 1)
    strict = ri > ci
    incl = ri >= ci
    eye = (lax.broadcasted_iota(jnp.int32, (n, n), 0) == lax.broadcasted_iota(jnp.int32, (n, n), 1)).astype(F32)

    rd, ad, bd, kd, bhd, khd, vd, g_last, where = [], [], [], [], [], [], [], [], []
    for s in range(nseq):
        lw = lw_ref[s]
        lg = sum(jnp.dot(tri, part, preferred_element_type=F32) for part in _split3(lw))
        lg_last = lg[ch - 1:ch, :]
        e_pos = jnp.exp(lg)
        e_neg = jnp.exp(-lg)
        e_prev = jnp.exp(lg - lw)
        e_tail = jnp.exp(lg_last - lg)
        kk = kk_ref[s]
        kka = kk * a_ref[s]
        kmod = k_ref[s]
        r_t = (r_ref[s] * e_pos).astype(BF16)
        a_t = (-kk * e_prev).astype(BF16)
        b_t = (kka * e_neg).astype(BF16)
        k_t = (kmod * e_neg).astype(BF16)
        b_h = (kka * e_tail).astype(BF16)
        k_h = (kmod * e_tail).astype(BF16)
        v_b = v_ref[s].astype(BF16)
        g_s = jnp.exp(lg_last)
        for p in range(n_pairs):
            sl = slice(p * LANES, (p + 1) * LANES)
            rd.append(stack(r_t[:, sl]))
            ad.append(stack(a_t[:, sl]))
            bd.append(stack(b_t[:, sl]))
            kd.append(stack(k_t[:, sl]))
            bhd.append(stack(b_h[:, sl]))
            khd.append(stack(k_h[:, sl]))
            vd.append(stack(v_b[:, sl]))
            g_last.append(g_s[:, sl])
            where.append((s, p, sl))
    units = range(len(where))

    ob = [_dot_nt(jnp.concatenate([ad[u], rd[u]], axis=0), bd[u]) for u in units]
    ok = [_dot_nt(jnp.concatenate([ad[u], rd[u]], axis=0), kd[u]) for u in units]
    a_ab = [jnp.where(strict, ob[u][:n], 0.0) for u in units]
    a_rb = [jnp.where(incl, ob[u][n:], 0.0).astype(BF16) for u in units]
    a_ak = [jnp.where(strict, ok[u][:n], 0.0) for u in units]
    a_rk = [jnp.where(incl, ok[u][n:], 0.0).astype(BF16) for u in units]
    akv = [_dot(a_ak[u], vd[u]) for u in units]

    tinv = [eye + a_ab[u] for u in units]
    pw = [_dot(a_ab[u], a_ab[u]) for u in units]
    for _ in range(int(math.log2(ch)) - 2):
        both = [_dot(jnp.concatenate([tinv[u], pw[u]], axis=0), pw[u]) for u in units]
        tinv = [tinv[u] + both[u][:n] for u in units]
        pw = [both[u][n:] for u in units]
    tinv = [tinv[u] + _dot(tinv[u], pw[u]) for u in units]

    w12 = [_dot(tinv[u], jnp.concatenate([ad[u], akv[u].astype(BF16)], axis=1)) for u in units]
    s_old = [state_ref[s, p] for s, p, _ in where]
    from_state = [_dot_nt(jnp.concatenate([w12[u][:, :LANES].astype(BF16), rd[u]], axis=0), s_old[u]) for u in units]
    uv = [jnp.concatenate([(from_state[u][:n] + w12[u][:, LANES:]).astype(BF16), vd[u]], axis=0) for u in units]
    y = [from_state[u][n:] + _dot(jnp.concatenate([a_rb[u], a_rk[u]], axis=1), uv[u]) for u in units]
    for u, (s, p, sl) in enumerate(where):
        y_ref[s, :, sl] = y[u][:ch] + y[u][ch:]
        state_ref[s, p] = s_old[u] * g_last[u] + _dot_tn(uv[u], jnp.concatenate([bhd[u], khd[u]], axis=0))


WKV_SEQS = 4


def _wkv(r, lw, k, v, kk, a, bsz, seq):
    width = r.shape[-1]
    nc = seq // CHUNK
    nseq = WKV_SEQS if bsz % WKV_SEQS == 0 else 1
    spec = pl.BlockSpec((nseq, CHUNK, width), lambda b, c: (b, c, 0))
    args = [t.reshape(bsz, seq, width) for t in (r, lw, k, v, kk, a)]
    y = pl.pallas_call(
        _wkv_body,
        out_shape=jax.ShapeDtypeStruct((bsz, seq, width), F32),
        grid=(bsz // nseq, nc),
        in_specs=[spec] * 6,
        out_specs=spec,
        scratch_shapes=[pltpu.VMEM((nseq, width // LANES, LANES, LANES), F32)],
        compiler_params=_params("parallel", "arbitrary"),
        name="wkv7_chunk",
    )(*args)
    return y.reshape(bsz * seq, width)


def _even_out_ffn_body(aw, y_ref, bonus_ref, gate_ref, yb_ref, x_ref, lnw_ref, lnb_ref, sel_ref, w_ref,
                       g_ref, wg_ref, wu_ref, wd_ref, o_ref):
    sel = sel_ref[...]
    y = y_ref[...]
    inv_n = 1.0 / HEAD_DIM
    d = y - _dot_sel(y, sel) * inv_n
    var = _dot_sel(d * d, sel) * inv_n
    yn = d * lax.rsqrt(var + GN_EPS) * lnw_ref[...] + lnb_ref[...]
    ya = ((yn + bonus_ref[...]) * gate_ref[...]).astype(BF16)
    x = (x_ref[...]
         + jnp.dot(ya, w_ref[0:aw, :], preferred_element_type=F32)
         + jnp.dot(yb_ref[...].astype(BF16), w_ref[aw:, :], preferred_element_type=F32))
    o_ref[...] = _swiglu_step(x, g_ref, wg_ref, wu_ref, wd_ref)


def _even_out_ffn(y, bonus, gate, yb, x2, lnw, lnb, sel, w, ffn_w, tm):
    t, d = x2.shape
    aw = y.shape[1]
    bw = yb.shape[1]
    row = lambda i: (i, 0)
    return pl.pallas_call(
        functools.partial(_even_out_ffn_body, aw),
        out_shape=jax.ShapeDtypeStruct((t, d), F32),
        grid=(t // tm,),
        in_specs=[pl.BlockSpec((tm, aw), row), pl.BlockSpec((tm, aw), row), pl.BlockSpec((tm, aw), row),
                  pl.BlockSpec((tm, bw), row), pl.BlockSpec((tm, d), row), _const_spec((1, aw)),
                  _const_spec((1, aw)), _const_spec((aw, aw)), _const_spec((d, d))]
                 + _ffn_specs(d, ffn_w[1].shape[1]),
        out_specs=pl.BlockSpec((tm, d), row),
        compiler_params=_params("parallel"),
        name="even_out_ffn",
    )(y, bonus, gate, yb, x2, lnw, lnb, sel, w, *ffn_w)


ODD_IN_SPLIT = 2


def _odd_in_body(d, x_ref, g_ref, w_ref, qc_ref, qs_ref, kc_ref, ks_ref, sel_ref, q_ref, k_ref, vt_ref):
    tm = x_ref.shape[0]
    sub = tm // ODD_IN_SPLIT
    sel = sel_ref[...]
    lane = lax.broadcasted_iota(jnp.int32, (sub, LANES), 1)
    first_half = (lane & (HEAD_DIM - 1)) < HEAD_DIM // 2
    zs = []
    for part in range(ODD_IN_SPLIT):
        rows = slice(part * sub, (part + 1) * sub)
        h = _rms(x_ref[rows, :], g_ref[...], 1e-6).astype(BF16)
        zs.append(jnp.dot(h, w_ref[...], preferred_element_type=F32))
    for part, z in enumerate(zs):
        rows = slice(part * sub, (part + 1) * sub)
        for blk in range(d // LANES):
            sl = slice(blk * LANES, (blk + 1) * LANES)
            for off, c_ref, s_ref, out_ref in ((0, qc_ref, qs_ref, q_ref), (d, kc_ref, ks_ref, k_ref)):
                xq = z[:, off + blk * LANES:off + (blk + 1) * LANES]
                ss = jnp.dot((xq * xq).astype(BF16), sel, preferred_element_type=F32)
                swapped = jnp.where(first_half, pltpu.roll(xq, LANES - HEAD_DIM // 2, 1),
                                    pltpu.roll(xq, HEAD_DIM // 2, 1))
                out = lax.rsqrt(ss + HEAD_DIM * 1e-6) * (xq * c_ref[rows, :] + swapped * s_ref[rows, :])
                out_ref[rows, sl] = out.astype(BF16)
        for hd in range(d // LANES):
            vt_ref[hd, :, rows] = z[:, 2 * d + hd * LANES:2 * d + (hd + 1) * LANES].T.astype(BF16)


def _odd_in(x2, bsz, seq, g, w, tabs, sel, tm):
    t, d = x2.shape
    nt = seq // tm
    heads = d // LANES
    row = lambda b, j: (b * nt + j, 0)
    pos = lambda b, j: (j, 0)
    out = jax.ShapeDtypeStruct((t, d), BF16)
    return pl.pallas_call(
        functools.partial(_odd_in_body, d),
        out_shape=(out, out, jax.ShapeDtypeStruct((bsz, heads, nt, LANES, tm), BF16)),
        grid=(bsz, nt),
        in_specs=[pl.BlockSpec((tm, d), row), _const_spec((1, d)), _const_spec((d, 3 * d))]
                 + [pl.BlockSpec((tm, LANES), pos)] * 4 + [_const_spec((LANES, LANES))],
        out_specs=(pl.BlockSpec((tm, d), row), pl.BlockSpec((tm, d), row),
                   pl.BlockSpec((None, heads, None, LANES, tm), lambda b, j: (b, 0, j, 0, 0))),
        compiler_params=_params("parallel", "parallel"),
        name="odd_in",
    )(x2, g.reshape(1, d), w, *tabs, sel)


ATTN_TQ = 512
ATTN_TK = 512
ATTN_GROUP = 3
SUM_ROWS = 16


def _attn_body(lam_init, q_ref, k_ref, vt_ref, lam_ref, sn_ref, o_ref, qd_buf, s_buf, p_buf, al_buf, m_ref, acc_ref):
    seq = q_ref.shape[0]
    tk = vt_ref.shape[-1]
    tq = qd_buf.shape[1] // 2
    per_q = tq // tk
    ones = jnp.ones((SUM_ROWS, tk), BF16)
    lv = lam_ref[...]
    lam = (jnp.exp(jnp.sum(lv[0:1] * lv[1:2], axis=-1, keepdims=True))
           - jnp.exp(jnp.sum(lv[2:3] * lv[3:4], axis=-1, keepdims=True)) + lam_init)
    lane = lax.broadcasted_iota(jnp.int32, (tq, LANES), 1)
    comp0 = lane < HEAD_DIM

    def scores(i, j, slot):
        if j == 0:
            q = q_ref[i * tq:(i + 1) * tq, :]
            zero = jnp.zeros_like(q)
            qd_buf[i & 1] = jnp.concatenate([jnp.where(comp0, q, zero), jnp.where(comp0, zero, q)], axis=0)
            m_ref[i & 1] = jnp.full(m_ref.shape[1:], -1e30, F32)
            acc_ref[i & 1] = jnp.zeros(acc_ref.shape[1:], F32)
        kj = k_ref[j * tk:(j + 1) * tk, :]
        s_buf[slot] = lax.dot_general(kj, qd_buf[i & 1], (((1,), (1,)), ((), ())),
                                      preferred_element_type=F32)

    def softmax(i, j, slot):
        s = s_buf[slot]
        first_diag = per_q * i
        if j >= first_diag:
            ki = lax.broadcasted_iota(jnp.int32, s.shape, 0) + (j - first_diag) * tk
            qi = lax.broadcasted_iota(jnp.int32, s.shape, 1) & (tq - 1)
            s = jnp.where(ki <= qi, s, -jnp.inf)
        m_old = m_ref[i & 1]
        m_new = jnp.maximum(m_old, jnp.max(s, axis=0, keepdims=True))
        m_ref[i & 1] = m_new
        al_buf[slot] = jnp.exp2(m_old - m_new)
        p_buf[slot] = jnp.exp2(s - m_new).astype(BF16)

    def values(i, j, slot):
        vj = jnp.concatenate([vt_ref[j], ones], axis=0)
        acc_ref[i & 1] = al_buf[slot] * acc_ref[i & 1] + jnp.dot(vj, p_buf[slot], preferred_element_type=F32)
        if j == per_q * (i + 1) - 1:
            acc = acc_ref[i & 1]
            o = acc[:LANES] / acc[LANES:LANES + 1]
            out = (o[:, :tq] - lam * o[:, tq:]).T
            out = _rms(out, sn_ref[...], 1e-5) * (1.0 - lam_init)
            o_ref[i * tq:(i + 1) * tq, :] = out.astype(BF16)

    blocks = [(i, j) for i in range(seq // tq) for j in range(per_q * (i + 1))]
    ring = s_buf.shape[0]
    group = ring // 2
    n_groups = -(-len(blocks) // group)
    for u in range(n_groups + 2):
        for stage, lag in ((values, 2), (softmax, 1), (scores, 0)):
            for g in range((u - lag) * group, (u - lag + 1) * group):
                if 0 <= g < len(blocks):
                    stage(*blocks[g], g % ring)


def _attn(q3, k3, vt5, lam_vecs, sub_norm, lam_init, tq):
    bsz, seq, d = q3.shape
    heads = d // LANES
    nkv, _, tk = vt5.shape[2:]
    assert tq % tk == 0 and seq % tq == 0
    ring = 2 * ATTN_GROUP
    head_rows = pl.BlockSpec((None, seq, LANES), lambda b, h: (b, 0, h))
    return pl.pallas_call(
        functools.partial(_attn_body, lam_init),
        out_shape=jax.ShapeDtypeStruct((bsz, seq, d), BF16),
        grid=(bsz, heads),
        in_specs=[head_rows, head_rows,
                  pl.BlockSpec((None, None, nkv, LANES, tk), lambda b, h: (b, h, 0, 0, 0)),
                  _const_spec((4, HEAD_DIM)), _const_spec((1, LANES))],
        out_specs=head_rows,
        scratch_shapes=[pltpu.VMEM((2, 2 * tq, LANES), BF16), pltpu.VMEM((ring, tk, 2 * tq), F32),
                        pltpu.VMEM((ring, tk, 2 * tq), BF16), pltpu.VMEM((ring, 1, 2 * tq), F32),
                        pltpu.VMEM((2, 1, 2 * tq), F32), pltpu.VMEM((2, LANES + SUM_ROWS, 2 * tq), F32)],
        compiler_params=_params("parallel", "parallel"),
        name="diff_attn",
    )(q3, k3, vt5, lam_vecs, sub_norm)


def _proj_ffn_body(o_ref, x_ref, w_ref, g_ref, wg_ref, wu_ref, wd_ref, out_ref):
    x = x_ref[...] + jnp.dot(o_ref[...], w_ref[...], preferred_element_type=F32)
    out_ref[...] = _swiglu_step(x, g_ref, wg_ref, wu_ref, wd_ref)


def _proj_ffn(o2, x2, w, ffn_w, tm):
    t, d = x2.shape
    row = pl.BlockSpec((tm, d), lambda i: (i, 0))
    return pl.pallas_call(
        _proj_ffn_body,
        out_shape=jax.ShapeDtypeStruct((t, d), F32),
        grid=(t // tm,),
        in_specs=[row, row, _const_spec((d, d))] + _ffn_specs(d, ffn_w[1].shape[1]),
        out_specs=row,
        compiler_params=_params("parallel"),
        name="proj_ffn",
    )(o2, x2, w, *ffn_w)


def _block_ones(n, blk):
    idx = jnp.arange(n) // blk
    return (idx[:, None] == idx[None, :]).astype(BF16)


def _rope_tables(seq):
    inv = 1.0 / (ROPE_THETA ** (jnp.arange(0, HEAD_DIM, 2, dtype=F32) / HEAD_DIM))
    ang = jnp.arange(seq, dtype=F32)[:, None] * inv[None, :]
    cos, sin = jnp.cos(ang), jnp.sin(ang)
    reps = LANES // HEAD_DIM
    return jnp.tile(jnp.concatenate([cos, cos], -1), (1, reps)), jnp.tile(jnp.concatenate([-sin, sin], -1), (1, reps))


def _qk_tables(cos, sin, norm, scale):
    c = scale * math.sqrt(HEAD_DIM)
    reps = LANES // HEAD_DIM
    gain = jnp.tile(norm, reps) * c
    gain_swapped = jnp.tile(jnp.roll(norm, HEAD_DIM // 2), reps) * c
    return cos * gain[None, :], sin * gain_swapped[None, :]


def _row_tile(seq):
    return min(256, seq)


def kernel(x, ffn_norm, ffn_w_gate, ffn_w_up, ffn_w_down, mix_norm, a_w_in, a_mu, a_w0, a_w2, a_a0, a_a2, a_g2, a_k_k, a_k_a, a_r_k, a_ln_w, a_ln_b, b_glu_bias, b_dw, b_dw_bias, b_ln_w, b_ln_b, e_w_out, c_w_in, c_q_norm, c_k_norm, c_lq1, c_lk1, c_lq2, c_lk2, c_sub_norm, c_w_out):
    bsz, seq, d = x.shape
    depth = ffn_norm.shape[0]
    t = bsz * seq
    tm = _row_tile(seq)
    tm_proj = min(FFN_TM, seq)
    x2 = x.reshape(t, d)
    cos, sin = _rope_tables(seq)

    def ffn_w(layer, which):
        return (ffn_norm[layer, which].reshape(1, d), ffn_w_gate[layer, which].astype(BF16),
                ffn_w_up[layer, which].astype(BF16), ffn_w_down[layer, which].astype(BF16))

    for layer in range(depth):
        x2 = _ffn(x2, ffn_w(layer, 0), tm_proj)
        j = layer // 2
        if layer % 2 == 0:
            aw = a_w0.shape[1]
            a_proj = a_mu.shape[1]
            row = lambda v: v.reshape(1, -1)
            sel = _block_ones(aw, HEAD_DIM)
            w2p = jnp.pad(a_w2[j], ((0, ICL_RANK), (0, 0)))
            w2h = w2p.astype(BF16)
            w2l = (w2p - w2h.astype(F32)).astype(BF16)
            a2p = jnp.pad(a_a2[j], ((DECAY_RANK, 0), (0, 0))).astype(BF16)
            prep_params = (row(a_mu[j]), row(a_w0[j]), w2h, w2l, row(a_a0[j]), a2p, a_g2[j].astype(BF16),
                           row(a_k_k[j]), row(a_k_a[j]), row(a_r_k[j]), sel)
            conv_params = (row(b_glu_bias[j]), jnp.pad(b_dw[j], ((0, CONV_HALO - CONV_WIDTH), (0, 0))),
                           row(b_dw_bias[j]), row(b_ln_w[j]), row(b_ln_b[j]))
            w_in = jnp.concatenate([a_w_in[j][:, a_proj:], a_w_in[j][:, :a_proj]], axis=1).astype(BF16)
            r, lw, k, v, kk, a, g, bonus, yb = _even_front(
                x2, bsz, seq, mix_norm[layer], w_in, a_proj, prep_params, conv_params, tm)
            y = _wkv(r, lw, k, v, kk, a, bsz, seq)
            x2 = _even_out_ffn(y, bonus, g, yb, x2, row(a_ln_w[j]), row(a_ln_b[j]), sel, e_w_out[j].astype(BF16),
                               ffn_w(layer, 1), tm_proj)
        else:
            lam_init = 0.8 - 0.6 * math.exp(-0.3 * layer)
            tabs = (_qk_tables(cos, sin, c_q_norm[j], ATTN_SCALE * math.log2(math.e))
                    + _qk_tables(cos, sin, c_k_norm[j], 1.0))
            q, k, vt5 = _odd_in(x2, bsz, seq, mix_norm[layer], c_w_in[j].astype(BF16), tabs,
                                _block_ones(LANES, HEAD_DIM), min(ATTN_TK, seq))
            lam_vecs = jnp.stack([c_lq1[j], c_lk1[j], c_lq2[j], c_lk2[j]])
            o = _attn(q.reshape(bsz, seq, d), k.reshape(bsz, seq, d), vt5, lam_vecs,
                      c_sub_norm[j].reshape(1, LANES), lam_init, ATTN_TQ)
            x2 = _proj_ffn(o.reshape(t, d), x2, c_w_out[j].astype(BF16), ffn_w(layer, 1), tm_proj)
    return x2.reshape(bsz, seq, d)
```

```python
import functools
import math

import jax
import jax.numpy as jnp
from jax import lax
from jax.experimental import pallas as pl
from jax.experimental.pallas import tpu as pltpu

F32 = jnp.float32
BF16 = jnp.bfloat16

HEAD_DIM = 64
LANES = 128
CONV_WIDTH = 31
CONV_HALO = 32
DECAY_RANK = 64
ICL_RANK = 64
GATE_RANK = 128
GN_EPS = 64e-5
ROPE_THETA = 10000.0
ATTN_SCALE = HEAD_DIM ** -0.5
CHUNK = 64
VMEM_LIMIT = 56 * 1024 * 1024


def _const_spec(shape):
    return pl.BlockSpec(shape, lambda *_: (0,) * len(shape), pipeline_mode=pl.Buffered(1))


def _params(*sem):
    return pltpu.CompilerParams(dimension_semantics=sem, vmem_limit_bytes=VMEM_LIMIT)


def _dot(a, b):
    return jnp.dot(a.astype(BF16), b.astype(BF16), preferred_element_type=F32)


def _dot_nt(a, b):
    return lax.dot_general(a.astype(BF16), b.astype(BF16), (((1,), (1,)), ((), ())),
                           preferred_element_type=F32)


def _dot_tn(a, b):
    return lax.dot_general(a.astype(BF16), b.astype(BF16), (((0,), (0,)), ((), ())),
                           preferred_element_type=F32)


def _split2(x):
    hi = x.astype(BF16)
    lo = (x - hi.astype(F32)).astype(BF16)
    return hi, lo


def _split3(x):
    hi = x.astype(BF16)
    r1 = x - hi.astype(F32)
    mid = r1.astype(BF16)
    lo = (r1 - mid.astype(F32)).astype(BF16)
    return hi, mid, lo


def _dot_sel(x, sel):
    return jnp.dot(x.astype(BF16), sel, preferred_element_type=F32)


def _dot_x3(a, b_hi, b_lo):
    a_hi, a_lo = _split2(a)
    return (jnp.dot(a_hi, b_hi, preferred_element_type=F32)
            + jnp.dot(a_lo, b_hi, preferred_element_type=F32)
            + jnp.dot(a_hi, b_lo, preferred_element_type=F32))


def _rms(x, g, eps):
    return x * lax.rsqrt(jnp.mean(x * x, axis=-1, keepdims=True) + eps) * g


def _sigmoid(x):
    return 1.0 / (1.0 + jnp.exp(-x))


FFN_TM = 512


def _swiglu_step(x, g_ref, wg_ref, wu_ref, wd_ref):
    h = _rms(x, g_ref[...], 1e-6).astype(BF16)
    gate = jnp.dot(h, wg_ref[...], preferred_element_type=F32)
    up = jnp.dot(h, wu_ref[...], preferred_element_type=F32)
    act = (gate * _sigmoid(gate) * up).astype(BF16)
    return x + 0.5 * jnp.dot(act, wd_ref[...], preferred_element_type=F32)


def _ffn_body(x_ref, g_ref, wg_ref, wu_ref, wd_ref, o_ref):
    o_ref[...] = _swiglu_step(x_ref[...], g_ref, wg_ref, wu_ref, wd_ref)


def _ffn_specs(d, f):
    return [_const_spec((1, d)), _const_spec((d, f)), _const_spec((d, f)), _const_spec((f, d))]


def _ffn(x2, ffn_w, tm):
    t, d = x2.shape
    row = pl.BlockSpec((tm, d), lambda i: (i, 0))
    return pl.pallas_call(
        _ffn_body,
        out_shape=jax.ShapeDtypeStruct((t, d), F32),
        grid=(t // tm,),
        in_specs=[row] + _ffn_specs(d, ffn_w[1].shape[1]),
        out_specs=row,
        compiler_params=_params("parallel"),
        name="ffn",
    )(x2, *ffn_w)


def _rwkv_prep_tile(aw, za, mu_ref, w0_ref, w2h_ref, w2l_ref, a0_ref, a2_ref, g2_ref, kk_ref_w, ka_ref,
                    rk_ref, sel_ref, r_out, lw_out, k_out, v_out, kk_out, a_out, g_out, bonus_out, prev_ref):
    tm = za.shape[0]
    row = lax.broadcasted_iota(jnp.int32, za.shape, 0)
    shifted = jnp.where(row == 0, prev_ref[0:1, :], pltpu.roll(za, 1, 0))
    prev_ref[0:1, :] = za[tm - 1:tm, :]
    xr = za + (shifted - za) * mu_ref[...]

    r = xr[:, 0:aw]
    k = xr[:, aw:2 * aw]
    v = xr[:, 2 * aw:3 * aw]
    lr = xr[:, 3 * aw:3 * aw + DECAY_RANK + ICL_RANK]
    gd = xr[:, 3 * aw + DECAY_RANK + ICL_RANK:]

    pre = w0_ref[...] + _dot_x3(jnp.tanh(lr), w2h_ref[...], w2l_ref[...])
    softplus_neg = jnp.maximum(-pre, 0.0) + jnp.log(1.0 + jnp.exp(-jnp.abs(pre)))
    w_log = -softplus_neg - 0.5
    lw_out[...] = -jnp.exp(w_log)

    a = _sigmoid(a0_ref[...] + jnp.dot(lr.astype(BF16), a2_ref[...], preferred_element_type=F32))
    g_out[...] = jnp.dot(_sigmoid(gd).astype(BF16), g2_ref[...], preferred_element_type=F32)

    sel = sel_ref[...]
    kk = k * kk_ref_w[...]
    kk = kk / jnp.maximum(jnp.sqrt(_dot_sel(kk * kk, sel)), 1e-12)
    kmod = k * (1.0 + (a - 1.0) * ka_ref[...])
    bonus_out[...] = _dot_sel(r * kmod * rk_ref[...], sel) * v
    r_out[...] = r
    k_out[...] = kmod
    v_out[...] = v
    kk_out[...] = kk
    a_out[...] = a


def _conv_tile(bw, zb, bias_ref, dw_ref, dwb_ref, lnw_ref, lnb_ref, y_ref, buf_ref, shift_ref):
    tm = zb.shape[0]
    u = zb + bias_ref[...]
    gl = u[:, :bw] * _sigmoid(u[:, bw:])
    buf_ref[0:CONV_HALO, :] = buf_ref[tm:tm + CONV_HALO, :]
    buf_ref[CONV_HALO:, :] = gl
    base = CONV_HALO - (CONV_WIDTH - 1)
    acc = jnp.zeros((tm, bw), F32)
    for shift in range(8):
        taps = [tap for tap in range(CONV_WIDTH) if (base + tap) % 8 == shift]
        if not taps:
            continue
        span = (base + taps[-1]) // 8 * 8
        if shift:
            shift_ref[shift - 1, 0:tm + span, :] = buf_ref[pl.ds(shift, tm + span), :]
        for tap in taps:
            off = (base + tap) // 8 * 8
            rows = shift_ref[shift - 1, off:off + tm, :] if shift else buf_ref[off:off + tm, :]
            acc = acc + dw_ref[tap:tap + 1, :] * rows
    c = acc + dwb_ref[...]
    mu = jnp.mean(c, axis=-1, keepdims=True)
    var = jnp.mean(jnp.square(c - mu), axis=-1, keepdims=True)
    y = (c - mu) * lax.rsqrt(var + 1e-5) * lnw_ref[...] + lnb_ref[...]
    y_ref[...] = y * _sigmoid(y)


N_PREP_IN = 11
N_CONV_IN = 5


def _even_front_body(aw, a_proj, bw, x_ref, g_ref, w_ref, *refs):
    prep_in, refs = refs[:N_PREP_IN], refs[N_PREP_IN:]
    conv_in, refs = refs[:N_CONV_IN], refs[N_CONV_IN:]
    prep_out, yb_out, (prev_ref, buf_ref, shift_ref) = refs[:8], refs[8], refs[9:]
    tm = x_ref.shape[0]

    @pl.when(pl.program_id(1) == 0)
    def _():
        prev_ref[...] = jnp.zeros_like(prev_ref)
        buf_ref[tm:tm + CONV_HALO, :] = jnp.zeros((CONV_HALO, bw), F32)

    h = _rms(x_ref[...], g_ref[...], 1e-6).astype(BF16)
    zb = jnp.dot(h, w_ref[:, :2 * bw], preferred_element_type=F32)
    za = jnp.dot(h, w_ref[:, 2 * bw:], preferred_element_type=F32)
    _conv_tile(bw, zb, *conv_in, yb_out, buf_ref, shift_ref)
    _rwkv_prep_tile(aw, za, *prep_in, *prep_out, prev_ref)


def _even_front(x2, bsz, seq, g, w, a_proj, prep_params, conv_params, tm):
    t, d = x2.shape
    n = w.shape[1]
    aw = prep_params[1].shape[1]
    bw = (n - a_proj) // 2
    nt = seq // tm
    row = lambda b, j: (b * nt + j, 0)
    whole = lambda arr: _const_spec(arr.shape)
    assert len(prep_params) == N_PREP_IN and len(conv_params) == N_CONV_IN
    return pl.pallas_call(
        functools.partial(_even_front_body, aw, a_proj, bw),
        out_shape=(jax.ShapeDtypeStruct((t, aw), F32),) * 8 + (jax.ShapeDtypeStruct((t, bw), F32),),
        grid=(bsz, nt),
        in_specs=[pl.BlockSpec((tm, d), row), _const_spec((1, d)), _const_spec((d, n))]
                 + [whole(p) for p in prep_params] + [whole(p) for p in conv_params],
        out_specs=(pl.BlockSpec((tm, aw), row),) * 8 + (pl.BlockSpec((tm, bw), row),),
        scratch_shapes=[pltpu.VMEM((8, a_proj), F32), pltpu.VMEM((tm + CONV_HALO, bw), F32),
                        pltpu.VMEM((7, tm + CONV_HALO - 8, bw), F32)],
        compiler_params=_params("parallel", "arbitrary"),
        name="even_front",
    )(x2, g.reshape(1, d), w, *prep_params, *conv_params)


def _wkv_body(r_ref, lw_ref, k_ref, v_ref, kk_ref, a_ref, y_ref, state_ref):
    nseq, ch, width = r_ref.shape
    n_pairs = width // LANES
    c = pl.program_id(1)

    @pl.when(c == 0)
    def _():
        state_ref[...] = jnp.zeros_like(state_ref)

    ti = lax.broadcasted_iota(jnp.int32, (ch, ch), 0)
    tj = lax.broadcasted_iota(jnp.int32, (ch, ch), 1)
    tri = (ti >= tj).astype(BF16)
    lane = lax.broadcasted_iota(jnp.int32, (ch, LANES), 1)
    head0 = lane < HEAD_DIM

    def stack(x):
        zero = jnp.zeros_like(x)
        return jnp.concatenate([jnp.where(head0, x, zero), jnp.where(head0, zero, x)], axis=0)

    n = 2 * ch
    ri = lax.broadcasted_iota(jnp.int32, (n, n), 0) & (ch - 1)
    ci = lax.broadcasted_iota(jnp.int32, (n, n), 1) & (ch - 1)
    strict = ri > ci
    incl = ri >= ci
    eye = (lax.broadcasted_iota(jnp.int32, (n, n), 0) == lax.broadcasted_iota(jnp.int32, (n, n), 1)).astype(F32)

    rd, ad, bd, kd, bhd, khd, vd, g_last, unit_pos = [], [], [], [], [], [], [], [], []
    for s in range(nseq):
        lw = lw_ref[s]
        lg = sum(jnp.dot(tri, part, preferred_element_type=F32) for part in _split3(lw))
        lg_last = lg[ch - 1:ch, :]
        e_pos = jnp.exp(lg)
        e_neg = jnp.exp(-lg)
        e_prev = jnp.exp(lg - lw)
        e_tail = jnp.exp(lg_last - lg)
        kk = kk_ref[s]
        kka = kk * a_ref[s]
        kmod = k_ref[s]
        r_t = (r_ref[s] * e_pos).astype(BF16)
        a_t = (-kk * e_prev).astype(BF16)
        b_t = (kka * e_neg).astype(BF16)
        k_t = (kmod * e_neg).astype(BF16)
        b_h = (kka * e_tail).astype(BF16)
        k_h = (kmod * e_tail).astype(BF16)
        v_b = v_ref[s].astype(BF16)
        g_s = jnp.exp(lg_last)
        for p in range(n_pairs):
            sl = slice(p * LANES, (p + 1) * LANES)
            rd.append(stack(r_t[:, sl]))
            ad.append(stack(a_t[:, sl]))
            bd.append(stack(b_t[:, sl]))
            kd.append(stack(k_t[:, sl]))
            bhd.append(stack(b_h[:, sl]))
            khd.append(stack(k_h[:, sl]))
            vd.append(stack(v_b[:, sl]))
            g_last.append(g_s[:, sl])
            unit_pos.append((s, p, sl))
    units = range(len(unit_pos))

    ob = [_dot_nt(jnp.concatenate([ad[u], rd[u]], axis=0), bd[u]) for u in units]
    ok = [_dot_nt(jnp.concatenate([ad[u], rd[u]], axis=0), kd[u]) for u in units]
    a_ab = [jnp.where(strict, ob[u][:n], 0.0) for u in units]
    a_rb = [jnp.where(incl, ob[u][n:], 0.0).astype(BF16) for u in units]
    a_ak = [jnp.where(strict, ok[u][:n], 0.0) for u in units]
    a_rk = [jnp.where(incl, ok[u][n:], 0.0).astype(BF16) for u in units]
    akv = [_dot(a_ak[u], vd[u]) for u in units]

    tinv = [eye + a_ab[u] for u in units]
    pw = [_dot(a_ab[u], a_ab[u]) for u in units]
    for _ in range(int(math.log2(ch)) - 2):
        both = [_dot(jnp.concatenate([tinv[u], pw[u]], axis=0), pw[u]) for u in units]
        tinv = [tinv[u] + both[u][:n] for u in units]
        pw = [both[u][n:] for u in units]
    tinv = [tinv[u] + _dot(tinv[u], pw[u]) for u in units]

    w12 = [_dot(tinv[u], jnp.concatenate([ad[u], akv[u].astype(BF16)], axis=1)) for u in units]
    s_old = [state_ref[s, p] for s, p, _ in unit_pos]
    from_state = [_dot_nt(jnp.concatenate([w12[u][:, :LANES].astype(BF16), rd[u]], axis=0), s_old[u]) for u in units]
    uv = [jnp.concatenate([(from_state[u][:n] + w12[u][:, LANES:]).astype(BF16), vd[u]], axis=0) for u in units]
    y = [from_state[u][n:] + _dot(jnp.concatenate([a_rb[u], a_rk[u]], axis=1), uv[u]) for u in units]
    for u, (s, p, sl) in enumerate(unit_pos):
        y_ref[s, :, sl] = y[u][:ch] + y[u][ch:]
        state_ref[s, p] = s_old[u] * g_last[u] + _dot_tn(uv[u], jnp.concatenate([bhd[u], khd[u]], axis=0))


WKV_SEQS = 8


def _wkv(r, lw, k, v, kk, a, bsz, seq):
    width = r.shape[-1]
    nc = seq // CHUNK
    nseq = WKV_SEQS if bsz % WKV_SEQS == 0 else 1
    spec = pl.BlockSpec((nseq, CHUNK, width), lambda b, c: (b, c, 0))
    args = [t.reshape(bsz, seq, width) for t in (r, lw, k, v, kk, a)]
    y = pl.pallas_call(
        _wkv_body,
        out_shape=jax.ShapeDtypeStruct((bsz, seq, width), F32),
        grid=(bsz // nseq, nc),
        in_specs=[spec] * 6,
        out_specs=spec,
        scratch_shapes=[pltpu.VMEM((nseq, width // LANES, LANES, LANES), F32)],
        compiler_params=_params("parallel", "arbitrary"),
        name="wkv7_chunk",
    )(*args)
    return y.reshape(bsz * seq, width)


def _even_out_ffn_body(aw, y_ref, bonus_ref, gate_ref, yb_ref, x_ref, lnw_ref, lnb_ref, sel_ref, w_ref,
                       g_ref, wg_ref, wu_ref, wd_ref, o_ref):
    sel = sel_ref[...]
    y = y_ref[...]
    inv_n = 1.0 / HEAD_DIM
    d = y - _dot_sel(y, sel) * inv_n
    var = _dot_sel(d * d, sel) * inv_n
    yn = d * lax.rsqrt(var + GN_EPS) * lnw_ref[...] + lnb_ref[...]
    ya = ((yn + bonus_ref[...]) * gate_ref[...]).astype(BF16)
    x = (x_ref[...]
         + jnp.dot(ya, w_ref[0:aw, :], preferred_element_type=F32)
         + jnp.dot(yb_ref[...].astype(BF16), w_ref[aw:, :], preferred_element_type=F32))
    o_ref[...] = _swiglu_step(x, g_ref, wg_ref, wu_ref, wd_ref)


def _even_out_ffn(y, bonus, gate, yb, x2, lnw, lnb, sel, w, ffn_w, tm):
    t, d = x2.shape
    aw = y.shape[1]
    bw = yb.shape[1]
    row = lambda i: (i, 0)
    return pl.pallas_call(
        functools.partial(_even_out_ffn_body, aw),
        out_shape=jax.ShapeDtypeStruct((t, d), F32),
        grid=(t // tm,),
        in_specs=[pl.BlockSpec((tm, aw), row), pl.BlockSpec((tm, aw), row), pl.BlockSpec((tm, aw), row),
                  pl.BlockSpec((tm, bw), row), pl.BlockSpec((tm, d), row), _const_spec((1, aw)),
                  _const_spec((1, aw)), _const_spec((aw, aw)), _const_spec((d, d))]
                 + _ffn_specs(d, ffn_w[1].shape[1]),
        out_specs=pl.BlockSpec((tm, d), row),
        compiler_params=_params("parallel"),
        name="even_out_ffn",
    )(y, bonus, gate, yb, x2, lnw, lnb, sel, w, *ffn_w)


ODD_IN_SPLIT = 2


def _odd_in_body(d, x_ref, g_ref, w_ref, qc_ref, qs_ref, kc_ref, ks_ref, sel_ref, q_ref, k_ref, vt_ref):
    tm = x_ref.shape[0]
    sub = tm // ODD_IN_SPLIT
    sel = sel_ref[...]
    lane = lax.broadcasted_iota(jnp.int32, (sub, LANES), 1)
    first_half = (lane & (HEAD_DIM - 1)) < HEAD_DIM // 2
    zs = []
    for part in range(ODD_IN_SPLIT):
        rows = slice(part * sub, (part + 1) * sub)
        h = _rms(x_ref[rows, :], g_ref[...], 1e-6).astype(BF16)
        zs.append(jnp.dot(h, w_ref[...], preferred_element_type=F32))
    for part, z in enumerate(zs):
        rows = slice(part * sub, (part + 1) * sub)
        for blk in range(d // LANES):
            sl = slice(blk * LANES, (blk + 1) * LANES)
            for off, c_ref, s_ref, out_ref in ((0, qc_ref, qs_ref, q_ref), (d, kc_ref, ks_ref, k_ref)):
                xq = z[:, off + blk * LANES:off + (blk + 1) * LANES]
                ss = jnp.dot((xq * xq).astype(BF16), sel, preferred_element_type=F32)
                swapped = jnp.where(first_half, pltpu.roll(xq, LANES - HEAD_DIM // 2, 1),
                                    pltpu.roll(xq, HEAD_DIM // 2, 1))
                out = lax.rsqrt(ss + HEAD_DIM * 1e-6) * (xq * c_ref[rows, :] + swapped * s_ref[rows, :])
                out_ref[rows, sl] = out.astype(BF16)
        for hd in range(d // LANES):
            vt_ref[hd, :, rows] = z[:, 2 * d + hd * LANES:2 * d + (hd + 1) * LANES].T.astype(BF16)


def _odd_in(x2, bsz, seq, g, w, tabs, sel, tm):
    t, d = x2.shape
    nt = seq // tm
    heads = d // LANES
    row = lambda b, j: (b * nt + j, 0)
    pos = lambda b, j: (j, 0)
    out = jax.ShapeDtypeStruct((t, d), BF16)
    return pl.pallas_call(
        functools.partial(_odd_in_body, d),
        out_shape=(out, out, jax.ShapeDtypeStruct((bsz, heads, nt, LANES, tm), BF16)),
        grid=(bsz, nt),
        in_specs=[pl.BlockSpec((tm, d), row), _const_spec((1, d)), _const_spec((d, 3 * d))]
                 + [pl.BlockSpec((tm, LANES), pos)] * 4 + [_const_spec((LANES, LANES))],
        out_specs=(pl.BlockSpec((tm, d), row), pl.BlockSpec((tm, d), row),
                   pl.BlockSpec((None, heads, None, LANES, tm), lambda b, j: (b, 0, j, 0, 0))),
        compiler_params=_params("parallel", "parallel"),
        name="odd_in",
    )(x2, g.reshape(1, d), w, *tabs, sel)


ATTN_TQ = 512
ATTN_TK = 512
ATTN_GROUP = 3
SUM_ROWS = 16


def _attn_body(lam_init, q_ref, k_ref, vt_ref, lam_ref, sn_ref, o_ref, qd_buf, s_buf, p_buf, al_buf, m_ref, acc_ref):
    seq = q_ref.shape[0]
    tk = vt_ref.shape[-1]
    tq = qd_buf.shape[1] // 2
    per_q = tq // tk
    ones = jnp.ones((SUM_ROWS, tk), BF16)
    lv = lam_ref[...]
    lam = (jnp.exp(jnp.sum(lv[0:1] * lv[1:2], axis=-1, keepdims=True))
           - jnp.exp(jnp.sum(lv[2:3] * lv[3:4], axis=-1, keepdims=True)) + lam_init)
    lane = lax.broadcasted_iota(jnp.int32, (tq, LANES), 1)
    comp0 = lane < HEAD_DIM

    def scores(i, j, slot):
        if j == 0:
            q = q_ref[i * tq:(i + 1) * tq, :]
            zero = jnp.zeros_like(q)
            qd_buf[i & 1] = jnp.concatenate([jnp.where(comp0, q, zero), jnp.where(comp0, zero, q)], axis=0)
            m_ref[i & 1] = jnp.full(m_ref.shape[1:], -1e30, F32)
            acc_ref[i & 1] = jnp.zeros(acc_ref.shape[1:], F32)
        kj = k_ref[j * tk:(j + 1) * tk, :]
        s_buf[slot] = lax.dot_general(kj, qd_buf[i & 1], (((1,), (1,)), ((), ())),
                                      preferred_element_type=F32)

    def softmax(i, j, slot):
        s = s_buf[slot]
        first_diag = per_q * i
        if j >= first_diag:
            ki = lax.broadcasted_iota(jnp.int32, s.shape, 0) + (j - first_diag) * tk
            qi = lax.broadcasted_iota(jnp.int32, s.shape, 1) & (tq - 1)
            s = jnp.where(ki <= qi, s, -jnp.inf)
        m_old = m_ref[i & 1]
        m_new = jnp.maximum(m_old, jnp.max(s, axis=0, keepdims=True))
        m_ref[i & 1] = m_new
        al_buf[slot] = jnp.exp2(m_old - m_new)
        p_buf[slot] = jnp.exp2(s - m_new).astype(BF16)

    def values(i, j, slot):
        vj = jnp.concatenate([vt_ref[j], ones], axis=0)
        acc_ref[i & 1] = al_buf[slot] * acc_ref[i & 1] + jnp.dot(vj, p_buf[slot], preferred_element_type=F32)
        if j == per_q * (i + 1) - 1:
            acc = acc_ref[i & 1]
            o = acc[:LANES] / acc[LANES:LANES + 1]
            out = (o[:, :tq] - lam * o[:, tq:]).T
            out = _rms(out, sn_ref[...], 1e-5) * (1.0 - lam_init)
            o_ref[i * tq:(i + 1) * tq, :] = out.astype(BF16)

    blocks = [(i, j) for i in range(seq // tq) for j in range(per_q * (i + 1))]
    ring = s_buf.shape[0]
    group = ring // 2
    n_groups = -(-len(blocks) // group)
    for u in range(n_groups + 2):
        for stage, lag in ((values, 2), (softmax, 1), (scores, 0)):
            for g in range((u - lag) * group, (u - lag + 1) * group):
                if 0 <= g < len(blocks):
                    stage(*blocks[g], g % ring)


def _attn(q3, k3, vt5, lam_vecs, sub_norm, lam_init, tq):
    bsz, seq, d = q3.shape
    heads = d // LANES
    nkv, _, tk = vt5.shape[2:]
    assert tq % tk == 0 and seq % tq == 0
    ring = 2 * ATTN_GROUP
    head_rows = pl.BlockSpec((None, seq, LANES), lambda b, h: (b, 0, h))
    return pl.pallas_call(
        functools.partial(_attn_body, lam_init),
        out_shape=jax.ShapeDtypeStruct((bsz, seq, d), BF16),
        grid=(bsz, heads),
        in_specs=[head_rows, head_rows,
                  pl.BlockSpec((None, None, nkv, LANES, tk), lambda b, h: (b, h, 0, 0, 0)),
                  _const_spec((4, HEAD_DIM)), _const_spec((1, LANES))],
        out_specs=head_rows,
        scratch_shapes=[pltpu.VMEM((2, 2 * tq, LANES), BF16), pltpu.VMEM((ring, tk, 2 * tq), F32),
                        pltpu.VMEM((ring, tk, 2 * tq), BF16), pltpu.VMEM((ring, 1, 2 * tq), F32),
                        pltpu.VMEM((2, 1, 2 * tq), F32), pltpu.VMEM((2, LANES + SUM_ROWS, 2 * tq), F32)],
        compiler_params=_params("parallel", "parallel"),
        name="diff_attn",
    )(q3, k3, vt5, lam_vecs, sub_norm)


def _proj_ffn_body(o_ref, x_ref, w_ref, g_ref, wg_ref, wu_ref, wd_ref, out_ref):
    x = x_ref[...] + jnp.dot(o_ref[...], w_ref[...], preferred_element_type=F32)
    out_ref[...] = _swiglu_step(x, g_ref, wg_ref, wu_ref, wd_ref)


def _proj_ffn(o2, x2, w, ffn_w, tm):
    t, d = x2.shape
    row = pl.BlockSpec((tm, d), lambda i: (i, 0))
    return pl.pallas_call(
        _proj_ffn_body,
        out_shape=jax.ShapeDtypeStruct((t, d), F32),
        grid=(t // tm,),
        in_specs=[row, row, _const_spec((d, d))] + _ffn_specs(d, ffn_w[1].shape[1]),
        out_specs=row,
        compiler_params=_params("parallel"),
        name="proj_ffn",
    )(o2, x2, w, *ffn_w)


def _block_ones(n, blk):
    idx = jnp.arange(n) // blk
    return (idx[:, None] == idx[None, :]).astype(BF16)


def _rope_tables(seq):
    inv = 1.0 / (ROPE_THETA ** (jnp.arange(0, HEAD_DIM, 2, dtype=F32) / HEAD_DIM))
    ang = jnp.arange(seq, dtype=F32)[:, None] * inv[None, :]
    cos, sin = jnp.cos(ang), jnp.sin(ang)
    reps = LANES // HEAD_DIM
    return jnp.tile(jnp.concatenate([cos, cos], -1), (1, reps)), jnp.tile(jnp.concatenate([-sin, sin], -1), (1, reps))


def _qk_tables(cos, sin, norm, scale):
    c = scale * math.sqrt(HEAD_DIM)
    reps = LANES // HEAD_DIM
    gain = jnp.tile(norm, reps) * c
    gain_swapped = jnp.tile(jnp.roll(norm, HEAD_DIM // 2), reps) * c
    return cos * gain[None, :], sin * gain_swapped[None, :]


EVEN_FRONT_TM = 256


def kernel(x, ffn_norm, ffn_w_gate, ffn_w_up, ffn_w_down, mix_norm, a_w_in, a_mu, a_w0, a_w2, a_a0, a_a2, a_g2, a_k_k, a_k_a, a_r_k, a_ln_w, a_ln_b, b_glu_bias, b_dw, b_dw_bias, b_ln_w, b_ln_b, e_w_out, c_w_in, c_q_norm, c_k_norm, c_lq1, c_lk1, c_lq2, c_lk2, c_sub_norm, c_w_out):
    bsz, seq, d = x.shape
    depth = ffn_norm.shape[0]
    t = bsz * seq
    tm = min(EVEN_FRONT_TM, seq)
    tm_proj = min(FFN_TM, seq)
    x2 = x.reshape(t, d)
    cos, sin = _rope_tables(seq)

    def ffn_w(layer, which):
        return (ffn_norm[layer, which].reshape(1, d), ffn_w_gate[layer, which].astype(BF16),
                ffn_w_up[layer, which].astype(BF16), ffn_w_down[layer, which].astype(BF16))

    for layer in range(depth):
        x2 = _ffn(x2, ffn_w(layer, 0), tm_proj)
        j = layer // 2
        if layer % 2 == 0:
            aw = a_w0.shape[1]
            a_proj = a_mu.shape[1]
            row = lambda v: v.reshape(1, -1)
            sel = _block_ones(aw, HEAD_DIM)
            w2p = jnp.pad(a_w2[j], ((0, ICL_RANK), (0, 0)))
            w2h = w2p.astype(BF16)
            w2l = (w2p - w2h.astype(F32)).astype(BF16)
            a2p = jnp.pad(a_a2[j], ((DECAY_RANK, 0), (0, 0))).astype(BF16)
            prep_params = (row(a_mu[j]), row(a_w0[j]), w2h, w2l, row(a_a0[j]), a2p, a_g2[j].astype(BF16),
                           row(a_k_k[j]), row(a_k_a[j]), row(a_r_k[j]), sel)
            conv_params = (row(b_glu_bias[j]), jnp.pad(b_dw[j], ((0, CONV_HALO - CONV_WIDTH), (0, 0))),
                           row(b_dw_bias[j]), row(b_ln_w[j]), row(b_ln_b[j]))
            w_in = jnp.concatenate([a_w_in[j][:, a_proj:], a_w_in[j][:, :a_proj]], axis=1).astype(BF16)
            r, lw, k, v, kk, a, g, bonus, yb = _even_front(
                x2, bsz, seq, mix_norm[layer], w_in, a_proj, prep_params, conv_params, tm)
            y = _wkv(r, lw, k, v, kk, a, bsz, seq)
            x2 = _even_out_ffn(y, bonus, g, yb, x2, row(a_ln_w[j]), row(a_ln_b[j]), sel, e_w_out[j].astype(BF16),
                               ffn_w(layer, 1), tm_proj)
        else:
            lam_init = 0.8 - 0.6 * math.exp(-0.3 * layer)
            tabs = (_qk_tables(cos, sin, c_q_norm[j], ATTN_SCALE * math.log2(math.e))
                    + _qk_tables(cos, sin, c_k_norm[j], 1.0))
            q, k, vt5 = _odd_in(x2, bsz, seq, mix_norm[layer], c_w_in[j].astype(BF16), tabs,
                                _block_ones(LANES, HEAD_DIM), min(ATTN_TK, seq))
            lam_vecs = jnp.stack([c_lq1[j], c_lk1[j], c_lq2[j], c_lk2[j]])
            o = _attn(q.reshape(bsz, seq, d), k.reshape(bsz, seq, d), vt5, lam_vecs,
                      c_sub_norm[j].reshape(1, LANES), lam_init, ATTN_TQ)
            x2 = _proj_ffn(o.reshape(t, d), x2, c_w_out[j].astype(BF16), ffn_w(layer, 1), tm_proj)
    return x2.reshape(bsz, seq, d)
```

```python
import functools
import math

import jax
import jax.numpy as jnp
from jax import lax
from jax.experimental import pallas as pl
from jax.experimental.pallas import tpu as pltpu

F32 = jnp.float32
BF16 = jnp.bfloat16

HEAD_DIM = 64
LANES = 128
CONV_WIDTH = 31
CONV_HALO = 32
DECAY_RANK = 64
ICL_RANK = 64
GATE_RANK = 128
GN_EPS = 64e-5
ROPE_THETA = 10000.0
ATTN_SCALE = HEAD_DIM ** -0.5
CHUNK = 64
VMEM_LIMIT = 56 * 1024 * 1024


def _const_spec(shape):
    return pl.BlockSpec(shape, lambda *_: (0,) * len(shape), pipeline_mode=pl.Buffered(1))


def _params(*sem):
    return pltpu.CompilerParams(dimension_semantics=sem, vmem_limit_bytes=VMEM_LIMIT)


def _dot(a, b):
    return jnp.dot(a.astype(BF16), b.astype(BF16), preferred_element_type=F32)


def _dot_nt(a, b):
    return lax.dot_general(a.astype(BF16), b.astype(BF16), (((1,), (1,)), ((), ())),
                           preferred_element_type=F32)


def _dot_tn(a, b):
    return lax.dot_general(a.astype(BF16), b.astype(BF16), (((0,), (0,)), ((), ())),
                           preferred_element_type=F32)


def _split2(x):
    hi = x.astype(BF16)
    lo = (x - hi.astype(F32)).astype(BF16)
    return hi, lo


def _split3(x):
    hi = x.astype(BF16)
    r1 = x - hi.astype(F32)
    mid = r1.astype(BF16)
    lo = (r1 - mid.astype(F32)).astype(BF16)
    return hi, mid, lo


def _dot_sel(x, sel):
    return jnp.dot(x.astype(BF16), sel, preferred_element_type=F32)


def _dot_x3(a, b_hi, b_lo):
    a_hi, a_lo = _split2(a)
    return (jnp.dot(a_hi, b_hi, preferred_element_type=F32)
            + jnp.dot(a_lo, b_hi, preferred_element_type=F32)
            + jnp.dot(a_hi, b_lo, preferred_element_type=F32))


def _rms(x, g, eps):
    return x * lax.rsqrt(jnp.mean(x * x, axis=-1, keepdims=True) + eps) * g


def _sigmoid(x):
    return 1.0 / (1.0 + jnp.exp(-x))


FFN_TM = 512


def _swiglu_step(x, g_ref, wg_ref, wu_ref, wd_ref):
    h = _rms(x, g_ref[...], 1e-6).astype(BF16)
    gate = jnp.dot(h, wg_ref[...], preferred_element_type=F32)
    up = jnp.dot(h, wu_ref[...], preferred_element_type=F32)
    act = (gate * _sigmoid(gate) * up).astype(BF16)
    return x + 0.5 * jnp.dot(act, wd_ref[...], preferred_element_type=F32)


def _ffn_body(x_ref, g_ref, wg_ref, wu_ref, wd_ref, o_ref):
    o_ref[...] = _swiglu_step(x_ref[...], g_ref, wg_ref, wu_ref, wd_ref)


def _ffn_specs(ffn_w, which):
    def pick(arr):
        return pl.BlockSpec((None, None) + arr.shape[2:], lambda *_: which + (0, 0), pipeline_mode=pl.Buffered(1))
    return [pick(arr) for arr in ffn_w]


def _ffn(x2, ffn_w, which, tm):
    t, d = x2.shape
    row = pl.BlockSpec((tm, d), lambda i: (i, 0))
    return pl.pallas_call(
        _ffn_body,
        out_shape=jax.ShapeDtypeStruct((t, d), F32),
        grid=(t // tm,),
        in_specs=[row] + _ffn_specs(ffn_w, which),
        out_specs=row,
        compiler_params=_params("parallel"),
        name="ffn",
    )(x2, *ffn_w)


def _rwkv_prep_tile(aw, za, mu_ref, w0_ref, w2h_ref, w2l_ref, a0_ref, a2_ref, g2_ref, kk_ref_w, ka_ref,
                    rk_ref, sel_ref, r_out, lw_out, k_out, v_out, kk_out, a_out, g_out, bonus_out, prev_ref):
    tm = za.shape[0]
    row = lax.broadcasted_iota(jnp.int32, za.shape, 0)
    shifted = jnp.where(row == 0, prev_ref[0:1, :], pltpu.roll(za, 1, 0))
    prev_ref[0:1, :] = za[tm - 1:tm, :]
    xr = za + (shifted - za) * mu_ref[...]

    r = xr[:, 0:aw]
    k = xr[:, aw:2 * aw]
    v = xr[:, 2 * aw:3 * aw]
    lr = xr[:, 3 * aw:3 * aw + DECAY_RANK + ICL_RANK]
    gd = xr[:, 3 * aw + DECAY_RANK + ICL_RANK:]

    pre = w0_ref[...] + _dot_x3(jnp.tanh(lr), w2h_ref[...], w2l_ref[...])
    softplus_neg = jnp.maximum(-pre, 0.0) + jnp.log(1.0 + jnp.exp(-jnp.abs(pre)))
    w_log = -softplus_neg - 0.5
    lw_out[...] = -jnp.exp(w_log)

    a = _sigmoid(a0_ref[...] + jnp.dot(lr.astype(BF16), a2_ref[...], preferred_element_type=F32))
    g_out[...] = jnp.dot(_sigmoid(gd).astype(BF16), g2_ref[...], preferred_element_type=F32)

    sel = sel_ref[...]
    kk = k * kk_ref_w[...]
    kk = kk / jnp.maximum(jnp.sqrt(_dot_sel(kk * kk, sel)), 1e-12)
    kmod = k * (1.0 + (a - 1.0) * ka_ref[...])
    bonus_out[...] = _dot_sel(r * kmod * rk_ref[...], sel) * v
    r_out[...] = r
    k_out[...] = kmod
    v_out[...] = v
    kk_out[...] = kk
    a_out[...] = a


def _conv_tile(bw, zb, bias_ref, dw_ref, dwb_ref, lnw_ref, lnb_ref, y_ref, buf_ref, shift_ref):
    tm = zb.shape[0]
    u = zb + bias_ref[...]
    gl = u[:, :bw] * _sigmoid(u[:, bw:])
    buf_ref[0:CONV_HALO, :] = buf_ref[tm:tm + CONV_HALO, :]
    buf_ref[CONV_HALO:, :] = gl
    base = CONV_HALO - (CONV_WIDTH - 1)
    acc = jnp.zeros((tm, bw), F32)
    for shift in range(8):
        taps = [tap for tap in range(CONV_WIDTH) if (base + tap) % 8 == shift]
        if not taps:
            continue
        span = (base + taps[-1]) // 8 * 8
        if shift:
            shift_ref[shift - 1, 0:tm + span, :] = buf_ref[pl.ds(shift, tm + span), :]
        for tap in taps:
            off = (base + tap) // 8 * 8
            rows = shift_ref[shift - 1, off:off + tm, :] if shift else buf_ref[off:off + tm, :]
            acc = acc + dw_ref[tap:tap + 1, :] * rows
    c = acc + dwb_ref[...]
    mu = jnp.mean(c, axis=-1, keepdims=True)
    var = jnp.mean(jnp.square(c - mu), axis=-1, keepdims=True)
    y = (c - mu) * lax.rsqrt(var + 1e-5) * lnw_ref[...] + lnb_ref[...]
    y_ref[...] = y * _sigmoid(y)


N_PREP_IN = 11
N_CONV_IN = 5


def _even_front_body(aw, a_proj, bw, x_ref, g_ref, w_ref, *refs):
    prep_in, refs = refs[:N_PREP_IN], refs[N_PREP_IN:]
    conv_in, refs = refs[:N_CONV_IN], refs[N_CONV_IN:]
    prep_out, yb_out, (prev_ref, buf_ref, shift_ref) = refs[:8], refs[8], refs[9:]
    tm = x_ref.shape[0]

    @pl.when(pl.program_id(1) == 0)
    def _():
        prev_ref[...] = jnp.zeros_like(prev_ref)
        buf_ref[tm:tm + CONV_HALO, :] = jnp.zeros((CONV_HALO, bw), F32)

    h = _rms(x_ref[...], g_ref[...], 1e-6).astype(BF16)
    zb = jnp.dot(h, w_ref[:, :2 * bw], preferred_element_type=F32)
    za = jnp.dot(h, w_ref[:, 2 * bw:], preferred_element_type=F32)
    _conv_tile(bw, zb, *conv_in, yb_out, buf_ref, shift_ref)
    _rwkv_prep_tile(aw, za, *prep_in, *prep_out, prev_ref)


def _even_front(x2, bsz, seq, g, w, a_proj, prep_params, conv_params, tm):
    t, d = x2.shape
    n = w.shape[1]
    aw = prep_params[1].shape[1]
    bw = (n - a_proj) // 2
    nt = seq // tm
    row = lambda b, j: (b * nt + j, 0)
    whole = lambda arr: _const_spec(arr.shape)
    assert len(prep_params) == N_PREP_IN and len(conv_params) == N_CONV_IN
    return pl.pallas_call(
        functools.partial(_even_front_body, aw, a_proj, bw),
        out_shape=(jax.ShapeDtypeStruct((t, aw), F32),) * 8 + (jax.ShapeDtypeStruct((t, bw), F32),),
        grid=(bsz, nt),
        in_specs=[pl.BlockSpec((tm, d), row), _const_spec((1, d)), _const_spec((d, n))]
                 + [whole(p) for p in prep_params] + [whole(p) for p in conv_params],
        out_specs=(pl.BlockSpec((tm, aw), row),) * 8 + (pl.BlockSpec((tm, bw), row),),
        scratch_shapes=[pltpu.VMEM((8, a_proj), F32), pltpu.VMEM((tm + CONV_HALO, bw), F32),
                        pltpu.VMEM((7, tm + CONV_HALO - 8, bw), F32)],
        compiler_params=_params("parallel", "arbitrary"),
        name="even_front",
    )(x2, g.reshape(1, d), w, *prep_params, *conv_params)


def _wkv_body(r_ref, lw_ref, k_ref, v_ref, kk_ref, a_ref, y_ref, state_ref):
    nseq, ch, width = r_ref.shape
    n_pairs = width // LANES
    c = pl.program_id(1)

    @pl.when(c == 0)
    def _():
        state_ref[...] = jnp.zeros_like(state_ref)

    ti = lax.broadcasted_iota(jnp.int32, (ch, ch), 0)
    tj = lax.broadcasted_iota(jnp.int32, (ch, ch), 1)
    tri = (ti >= tj).astype(BF16)
    lane = lax.broadcasted_iota(jnp.int32, (ch, LANES), 1)
    head0 = lane < HEAD_DIM

    def stack(x):
        zero = jnp.zeros_like(x)
        return jnp.concatenate([jnp.where(head0, x, zero), jnp.where(head0, zero, x)], axis=0)

    n = 2 * ch
    ri = lax.broadcasted_iota(jnp.int32, (n, n), 0) & (ch - 1)
    ci = lax.broadcasted_iota(jnp.int32, (n, n), 1) & (ch - 1)
    strict = ri > ci
    incl = ri >= ci
    eye = (lax.broadcasted_iota(jnp.int32, (n, n), 0) == lax.broadcasted_iota(jnp.int32, (n, n), 1)).astype(F32)

    rd, ad, bd, kd, bhd, khd, vd, g_last, unit_pos = [], [], [], [], [], [], [], [], []
    for s in range(nseq):
        lw = lw_ref[s]
        lg = sum(jnp.dot(tri, part, preferred_element_type=F32) for part in _split3(lw))
        lg_last = lg[ch - 1:ch, :]
        e_pos = jnp.exp(lg)
        e_neg = jnp.exp(-lg)
        e_prev = jnp.exp(lg - lw)
        e_tail = jnp.exp(lg_last - lg)
        kk = kk_ref[s]
        kka = kk * a_ref[s]
        kmod = k_ref[s]
        r_t = (r_ref[s] * e_pos).astype(BF16)
        a_t = (-kk * e_prev).astype(BF16)
        b_t = (kka * e_neg).astype(BF16)
        k_t = (kmod * e_neg).astype(BF16)
        b_h = (kka * e_tail).astype(BF16)
        k_h = (kmod * e_tail).astype(BF16)
        v_b = v_ref[s].astype(BF16)
        g_s = jnp.exp(lg_last)
        for p in range(n_pairs):
            sl = slice(p * LANES, (p + 1) * LANES)
            rd.append(stack(r_t[:, sl]))
            ad.append(stack(a_t[:, sl]))
            bd.append(stack(b_t[:, sl]))
            kd.append(stack(k_t[:, sl]))
            bhd.append(stack(b_h[:, sl]))
            khd.append(stack(k_h[:, sl]))
            vd.append(stack(v_b[:, sl]))
            g_last.append(g_s[:, sl])
            unit_pos.append((s, p, sl))
    units = range(len(unit_pos))

    ob = [_dot_nt(jnp.concatenate([ad[u], rd[u]], axis=0), bd[u]) for u in units]
    ok = [_dot_nt(jnp.concatenate([ad[u], rd[u]], axis=0), kd[u]) for u in units]
    a_ab = [jnp.where(strict, ob[u][:n], 0.0) for u in units]
    a_rb = [jnp.where(incl, ob[u][n:], 0.0).astype(BF16) for u in units]
    a_ak = [jnp.where(strict, ok[u][:n], 0.0) for u in units]
    a_rk = [jnp.where(incl, ok[u][n:], 0.0).astype(BF16) for u in units]
    akv = [_dot(a_ak[u], vd[u]) for u in units]

    tinv = [eye + a_ab[u] for u in units]
    pw = [_dot(a_ab[u], a_ab[u]) for u in units]
    for _ in range(int(math.log2(ch)) - 2):
        both = [_dot(jnp.concatenate([tinv[u], pw[u]], axis=0), pw[u]) for u in units]
        tinv = [tinv[u] + both[u][:n] for u in units]
        pw = [both[u][n:] for u in units]
    tinv = [tinv[u] + _dot(tinv[u], pw[u]) for u in units]

    w12 = [_dot(tinv[u], jnp.concatenate([ad[u], akv[u].astype(BF16)], axis=1)) for u in units]
    s_old = [state_ref[s, p] for s, p, _ in unit_pos]
    from_state = [_dot_nt(jnp.concatenate([w12[u][:, :LANES].astype(BF16), rd[u]], axis=0), s_old[u]) for u in units]
    uv = [jnp.concatenate([(from_state[u][:n] + w12[u][:, LANES:]).astype(BF16), vd[u]], axis=0) for u in units]
    y = [from_state[u][n:] + _dot(jnp.concatenate([a_rb[u], a_rk[u]], axis=1), uv[u]) for u in units]
    for u, (s, p, sl) in enumerate(unit_pos):
        y_ref[s, :, sl] = y[u][:ch] + y[u][ch:]
        state_ref[s, p] = s_old[u] * g_last[u] + _dot_tn(uv[u], jnp.concatenate([bhd[u], khd[u]], axis=0))


WKV_SEQS = 8


def _wkv(r, lw, k, v, kk, a, bsz, seq):
    width = r.shape[-1]
    nc = seq // CHUNK
    nseq = WKV_SEQS if bsz % WKV_SEQS == 0 else 1
    spec = pl.BlockSpec((nseq, CHUNK, width), lambda b, c: (b, c, 0))
    args = [t.reshape(bsz, seq, width) for t in (r, lw, k, v, kk, a)]
    y = pl.pallas_call(
        _wkv_body,
        out_shape=jax.ShapeDtypeStruct((bsz, seq, width), F32),
        grid=(bsz // nseq, nc),
        in_specs=[spec] * 6,
        out_specs=spec,
        scratch_shapes=[pltpu.VMEM((nseq, width // LANES, LANES, LANES), F32)],
        compiler_params=_params("parallel", "arbitrary"),
        name="wkv7_chunk",
    )(*args)
    return y.reshape(bsz * seq, width)


def _even_out_ffn_body(aw, y_ref, bonus_ref, gate_ref, yb_ref, x_ref, lnw_ref, lnb_ref, sel_ref, w_ref,
                       g_ref, wg_ref, wu_ref, wd_ref, o_ref):
    sel = sel_ref[...]
    y = y_ref[...]
    inv_n = 1.0 / HEAD_DIM
    d = y - _dot_sel(y, sel) * inv_n
    var = _dot_sel(d * d, sel) * inv_n
    yn = d * lax.rsqrt(var + GN_EPS) * lnw_ref[...] + lnb_ref[...]
    ya = ((yn + bonus_ref[...]) * gate_ref[...]).astype(BF16)
    x = (x_ref[...]
         + jnp.dot(ya, w_ref[0:aw, :], preferred_element_type=F32)
         + jnp.dot(yb_ref[...].astype(BF16), w_ref[aw:, :], preferred_element_type=F32))
    o_ref[...] = _swiglu_step(x, g_ref, wg_ref, wu_ref, wd_ref)


def _even_out_ffn(y, bonus, gate, yb, x2, lnw, lnb, sel, w, ffn_w, which, tm):
    t, d = x2.shape
    aw = y.shape[1]
    bw = yb.shape[1]
    row = lambda i: (i, 0)
    return pl.pallas_call(
        functools.partial(_even_out_ffn_body, aw),
        out_shape=jax.ShapeDtypeStruct((t, d), F32),
        grid=(t // tm,),
        in_specs=[pl.BlockSpec((tm, aw), row), pl.BlockSpec((tm, aw), row), pl.BlockSpec((tm, aw), row),
                  pl.BlockSpec((tm, bw), row), pl.BlockSpec((tm, d), row), _const_spec((1, aw)),
                  _const_spec((1, aw)), _const_spec((aw, aw)), _const_spec((d, d))]
                 + _ffn_specs(ffn_w, which),
        out_specs=pl.BlockSpec((tm, d), row),
        compiler_params=_params("parallel"),
        name="even_out_ffn",
    )(y, bonus, gate, yb, x2, lnw, lnb, sel, w, *ffn_w)


ODD_IN_SPLIT = 2


def _odd_in_body(d, x_ref, g_ref, w_ref, qc_ref, qs_ref, kc_ref, ks_ref, sel_ref, q_ref, k_ref, vt_ref):
    tm = x_ref.shape[0]
    sub = tm // ODD_IN_SPLIT
    sel = sel_ref[...]
    lane = lax.broadcasted_iota(jnp.int32, (sub, LANES), 1)
    first_half = (lane & (HEAD_DIM - 1)) < HEAD_DIM // 2
    zs = []
    for part in range(ODD_IN_SPLIT):
        rows = slice(part * sub, (part + 1) * sub)
        h = _rms(x_ref[rows, :], g_ref[...], 1e-6).astype(BF16)
        zs.append(jnp.dot(h, w_ref[...], preferred_element_type=F32))
    for part, z in enumerate(zs):
        rows = slice(part * sub, (part + 1) * sub)
        for blk in range(d // LANES):
            sl = slice(blk * LANES, (blk + 1) * LANES)
            for off, c_ref, s_ref, out_ref in ((0, qc_ref, qs_ref, q_ref), (d, kc_ref, ks_ref, k_ref)):
                xq = z[:, off + blk * LANES:off + (blk + 1) * LANES]
                ss = jnp.dot((xq * xq).astype(BF16), sel, preferred_element_type=F32)
                swapped = jnp.where(first_half, pltpu.roll(xq, LANES - HEAD_DIM // 2, 1),
                                    pltpu.roll(xq, HEAD_DIM // 2, 1))
                out = lax.rsqrt(ss + HEAD_DIM * 1e-6) * (xq * c_ref[rows, :] + swapped * s_ref[rows, :])
                out_ref[rows, sl] = out.astype(BF16)
        for hd in range(d // LANES):
            vt_ref[hd, :, rows] = z[:, 2 * d + hd * LANES:2 * d + (hd + 1) * LANES].T.astype(BF16)


def _odd_in(x2, bsz, seq, g, w, tabs, sel, tm):
    t, d = x2.shape
    nt = seq // tm
    heads = d // LANES
    row = lambda b, j: (b * nt + j, 0)
    pos = lambda b, j: (j, 0)
    out = jax.ShapeDtypeStruct((t, d), BF16)
    return pl.pallas_call(
        functools.partial(_odd_in_body, d),
        out_shape=(out, out, jax.ShapeDtypeStruct((bsz, heads, nt, LANES, tm), BF16)),
        grid=(bsz, nt),
        in_specs=[pl.BlockSpec((tm, d), row), _const_spec((1, d)), _const_spec((d, 3 * d))]
                 + [pl.BlockSpec((tm, LANES), pos)] * 4 + [_const_spec((LANES, LANES))],
        out_specs=(pl.BlockSpec((tm, d), row), pl.BlockSpec((tm, d), row),
                   pl.BlockSpec((None, heads, None, LANES, tm), lambda b, j: (b, 0, j, 0, 0))),
        compiler_params=_params("parallel", "parallel"),
        name="odd_in",
    )(x2, g.reshape(1, d), w, *tabs, sel)


ATTN_TQ = 512
ATTN_TK = 512
ATTN_GROUP = 3
SUM_ROWS = 16


def _attn_body(lam_init, q_ref, k_ref, vt_ref, lam_ref, sn_ref, o_ref, qd_buf, s_buf, p_buf, al_buf, m_ref, acc_ref):
    seq = q_ref.shape[0]
    tk = vt_ref.shape[-1]
    tq = qd_buf.shape[1] // 2
    per_q = tq // tk
    ones = jnp.ones((SUM_ROWS, tk), BF16)
    lv = lam_ref[...]
    lam = (jnp.exp(jnp.sum(lv[0:1] * lv[1:2], axis=-1, keepdims=True))
           - jnp.exp(jnp.sum(lv[2:3] * lv[3:4], axis=-1, keepdims=True)) + lam_init)
    lane = lax.broadcasted_iota(jnp.int32, (tq, LANES), 1)
    comp0 = lane < HEAD_DIM

    def scores(i, j, slot):
        if j == 0:
            q = q_ref[i * tq:(i + 1) * tq, :]
            zero = jnp.zeros_like(q)
            qd_buf[i & 1] = jnp.concatenate([jnp.where(comp0, q, zero), jnp.where(comp0, zero, q)], axis=0)
            m_ref[i & 1] = jnp.full(m_ref.shape[1:], -1e30, F32)
            acc_ref[i & 1] = jnp.zeros(acc_ref.shape[1:], F32)
        kj = k_ref[j * tk:(j + 1) * tk, :]
        s_buf[slot] = lax.dot_general(kj, qd_buf[i & 1], (((1,), (1,)), ((), ())),
                                      preferred_element_type=F32)

    def softmax(i, j, slot):
        s = s_buf[slot]
        first_diag = per_q * i
        if j >= first_diag:
            ki = lax.broadcasted_iota(jnp.int32, s.shape, 0) + (j - first_diag) * tk
            qi = lax.broadcasted_iota(jnp.int32, s.shape, 1) & (tq - 1)
            s = jnp.where(ki <= qi, s, -jnp.inf)
        m_old = m_ref[i & 1]
        m_new = jnp.maximum(m_old, jnp.max(s, axis=0, keepdims=True))
        m_ref[i & 1] = m_new
        al_buf[slot] = jnp.exp2(m_old - m_new)
        p_buf[slot] = jnp.exp2(s - m_new).astype(BF16)

    def values(i, j, slot):
        vj = jnp.concatenate([vt_ref[j], ones], axis=0)
        acc_ref[i & 1] = al_buf[slot] * acc_ref[i & 1] + jnp.dot(vj, p_buf[slot], preferred_element_type=F32)
        if j == per_q * (i + 1) - 1:
            acc = acc_ref[i & 1]
            o = acc[:LANES] / acc[LANES:LANES + 1]
            out = (o[:, :tq] - lam * o[:, tq:]).T
            out = _rms(out, sn_ref[...], 1e-5) * (1.0 - lam_init)
            o_ref[i * tq:(i + 1) * tq, :] = out.astype(BF16)

    blocks = [(i, j) for i in range(seq // tq) for j in range(per_q * (i + 1))]
    ring = s_buf.shape[0]
    group = ring // 2
    n_groups = -(-len(blocks) // group)
    for u in range(n_groups + 2):
        for stage, lag in ((values, 2), (softmax, 1), (scores, 0)):
            for g in range((u - lag) * group, (u - lag + 1) * group):
                if 0 <= g < len(blocks):
                    stage(*blocks[g], g % ring)


def _attn(q3, k3, vt5, lam_vecs, sub_norm, lam_init, tq):
    bsz, seq, d = q3.shape
    heads = d // LANES
    nkv, _, tk = vt5.shape[2:]
    assert tq % tk == 0 and seq % tq == 0
    ring = 2 * ATTN_GROUP
    head_rows = pl.BlockSpec((None, seq, LANES), lambda b, h: (b, 0, h))
    return pl.pallas_call(
        functools.partial(_attn_body, lam_init),
        out_shape=jax.ShapeDtypeStruct((bsz, seq, d), BF16),
        grid=(bsz, heads),
        in_specs=[head_rows, head_rows,
                  pl.BlockSpec((None, None, nkv, LANES, tk), lambda b, h: (b, h, 0, 0, 0)),
                  _const_spec((4, HEAD_DIM)), _const_spec((1, LANES))],
        out_specs=head_rows,
        scratch_shapes=[pltpu.VMEM((2, 2 * tq, LANES), BF16), pltpu.VMEM((ring, tk, 2 * tq), F32),
                        pltpu.VMEM((ring, tk, 2 * tq), BF16), pltpu.VMEM((ring, 1, 2 * tq), F32),
                        pltpu.VMEM((2, 1, 2 * tq), F32), pltpu.VMEM((2, LANES + SUM_ROWS, 2 * tq), F32)],
        compiler_params=_params("parallel", "parallel"),
        name="diff_attn",
    )(q3, k3, vt5, lam_vecs, sub_norm)


def _proj_ffn_body(o_ref, x_ref, w_ref, g_ref, wg_ref, wu_ref, wd_ref, out_ref):
    x = x_ref[...] + jnp.dot(o_ref[...], w_ref[...], preferred_element_type=F32)
    out_ref[...] = _swiglu_step(x, g_ref, wg_ref, wu_ref, wd_ref)


def _proj_ffn(o2, x2, w, ffn_w, which, tm):
    t, d = x2.shape
    row = pl.BlockSpec((tm, d), lambda i: (i, 0))
    return pl.pallas_call(
        _proj_ffn_body,
        out_shape=jax.ShapeDtypeStruct((t, d), F32),
        grid=(t // tm,),
        in_specs=[row, row, _const_spec((d, d))] + _ffn_specs(ffn_w, which),
        out_specs=row,
        compiler_params=_params("parallel"),
        name="proj_ffn",
    )(o2, x2, w, *ffn_w)


def _block_ones(n, blk):
    idx = jnp.arange(n) // blk
    return (idx[:, None] == idx[None, :]).astype(BF16)


def _rope_tables(seq):
    inv = 1.0 / (ROPE_THETA ** (jnp.arange(0, HEAD_DIM, 2, dtype=F32) / HEAD_DIM))
    ang = jnp.arange(seq, dtype=F32)[:, None] * inv[None, :]
    cos, sin = jnp.cos(ang), jnp.sin(ang)
    reps = LANES // HEAD_DIM
    return jnp.tile(jnp.concatenate([cos, cos], -1), (1, reps)), jnp.tile(jnp.concatenate([-sin, sin], -1), (1, reps))


def _qk_tables(cos, sin, norm, scale):
    c = scale * math.sqrt(HEAD_DIM)
    reps = LANES // HEAD_DIM
    gain = jnp.tile(norm, reps) * c
    gain_swapped = jnp.tile(jnp.roll(norm, HEAD_DIM // 2), reps) * c
    return cos * gain[None, :], sin * gain_swapped[None, :]


EVEN_FRONT_TM = 512


def kernel(x, ffn_norm, ffn_w_gate, ffn_w_up, ffn_w_down, mix_norm, a_w_in, a_mu, a_w0, a_w2, a_a0, a_a2, a_g2, a_k_k, a_k_a, a_r_k, a_ln_w, a_ln_b, b_glu_bias, b_dw, b_dw_bias, b_ln_w, b_ln_b, e_w_out, c_w_in, c_q_norm, c_k_norm, c_lq1, c_lk1, c_lq2, c_lk2, c_sub_norm, c_w_out):
    bsz, seq, d = x.shape
    depth = ffn_norm.shape[0]
    t = bsz * seq
    tm = min(EVEN_FRONT_TM, seq)
    tm_proj = min(FFN_TM, seq)
    x2 = x.reshape(t, d)
    cos, sin = _rope_tables(seq)

    ffn_w = (ffn_norm.reshape(depth, 2, 1, d), ffn_w_gate.astype(BF16), ffn_w_up.astype(BF16),
             ffn_w_down.astype(BF16))

    for layer in range(depth):
        x2 = _ffn(x2, ffn_w, (layer, 0), tm_proj)
        j = layer // 2
        if layer % 2 == 0:
            aw = a_w0.shape[1]
            a_proj = a_mu.shape[1]
            row = lambda v: v.reshape(1, -1)
            sel = _block_ones(aw, HEAD_DIM)
            w2p = jnp.pad(a_w2[j], ((0, ICL_RANK), (0, 0)))
            w2h = w2p.astype(BF16)
            w2l = (w2p - w2h.astype(F32)).astype(BF16)
            a2p = jnp.pad(a_a2[j], ((DECAY_RANK, 0), (0, 0))).astype(BF16)
            prep_params = (row(a_mu[j]), row(a_w0[j]), w2h, w2l, row(a_a0[j]), a2p, a_g2[j].astype(BF16),
                           row(a_k_k[j]), row(a_k_a[j]), row(a_r_k[j]), sel)
            conv_params = (row(b_glu_bias[j]), jnp.pad(b_dw[j], ((0, CONV_HALO - CONV_WIDTH), (0, 0))),
                           row(b_dw_bias[j]), row(b_ln_w[j]), row(b_ln_b[j]))
            w_in = jnp.concatenate([a_w_in[j][:, a_proj:], a_w_in[j][:, :a_proj]], axis=1).astype(BF16)
            r, lw, k, v, kk, a, g, bonus, yb = _even_front(
                x2, bsz, seq, mix_norm[layer], w_in, a_proj, prep_params, conv_params, tm)
            y = _wkv(r, lw, k, v, kk, a, bsz, seq)
            x2 = _even_out_ffn(y, bonus, g, yb, x2, row(a_ln_w[j]), row(a_ln_b[j]), sel, e_w_out[j].astype(BF16),
                               ffn_w, (layer, 1), tm_proj)
        else:
            lam_init = 0.8 - 0.6 * math.exp(-0.3 * layer)
            tabs = (_qk_tables(cos, sin, c_q_norm[j], ATTN_SCALE * math.log2(math.e))
                    + _qk_tables(cos, sin, c_k_norm[j], 1.0))
            q, k, vt5 = _odd_in(x2, bsz, seq, mix_norm[layer], c_w_in[j].astype(BF16), tabs,
                                _block_ones(LANES, HEAD_DIM), min(ATTN_TK, seq))
            lam_vecs = jnp.stack([c_lq1[j], c_lk1[j], c_lq2[j], c_lk2[j]])
            o = _attn(q.reshape(bsz, seq, d), k.reshape(bsz, seq, d), vt5, lam_vecs,
                      c_sub_norm[j].reshape(1, LANES), lam_init, ATTN_TQ)
            x2 = _proj_ffn(o.reshape(t, d), x2, c_w_out[j].astype(BF16), ffn_w, (layer, 1), tm_proj)
    return x2.reshape(bsz, seq, d)
```

```python
import functools
import math

import jax
import jax.numpy as jnp
from jax import lax
from jax.experimental import pallas as pl
from jax.experimental.pallas import tpu as pltpu

F32 = jnp.float32
BF16 = jnp.bfloat16

HEAD_DIM = 64
LANES = 128
CONV_WIDTH = 31
CONV_HALO = 32
DECAY_RANK = 64
ICL_RANK = 64
GATE_RANK = 128
GN_EPS = 64e-5
ROPE_THETA = 10000.0
ATTN_SCALE = HEAD_DIM ** -0.5
CHUNK = 64
VMEM_LIMIT = 56 * 1024 * 1024


def _const_spec(shape):
    return pl.BlockSpec(shape, lambda *_: (0,) * len(shape), pipeline_mode=pl.Buffered(1))


def _params(*sem):
    return pltpu.CompilerParams(dimension_semantics=sem, vmem_limit_bytes=VMEM_LIMIT)


def _dot(a, b):
    return jnp.dot(a.astype(BF16), b.astype(BF16), preferred_element_type=F32)


def _dot_nt(a, b):
    return lax.dot_general(a.astype(BF16), b.astype(BF16), (((1,), (1,)), ((), ())),
                           preferred_element_type=F32)


def _dot_tn(a, b):
    return lax.dot_general(a.astype(BF16), b.astype(BF16), (((0,), (0,)), ((), ())),
                           preferred_element_type=F32)


def _split2(x):
    hi = x.astype(BF16)
    lo = (x - hi.astype(F32)).astype(BF16)
    return hi, lo


def _split3(x):
    hi = x.astype(BF16)
    r1 = x - hi.astype(F32)
    mid = r1.astype(BF16)
    lo = (r1 - mid.astype(F32)).astype(BF16)
    return hi, mid, lo


def _dot_sel(x, sel):
    return jnp.dot(x.astype(BF16), sel, preferred_element_type=F32)


def _dot_x3(a, b_hi, b_lo):
    a_hi, a_lo = _split2(a)
    return (jnp.dot(a_hi, b_hi, preferred_element_type=F32)
            + jnp.dot(a_lo, b_hi, preferred_element_type=F32)
            + jnp.dot(a_hi, b_lo, preferred_element_type=F32))


def _rms(x, g, eps):
    return x * lax.rsqrt(jnp.mean(x * x, axis=-1, keepdims=True) + eps) * g


def _sigmoid(x):
    return 1.0 / (1.0 + jnp.exp(-x))


FFN_TM = 512


def _swiglu_step(x, g_ref, wg_ref, wu_ref, wd_ref):
    h = _rms(x, g_ref[...], 1e-6).astype(BF16)
    gate = jnp.dot(h, wg_ref[...], preferred_element_type=F32)
    up = jnp.dot(h, wu_ref[...], preferred_element_type=F32)
    act = (gate * _sigmoid(gate) * up).astype(BF16)
    return x + 0.5 * jnp.dot(act, wd_ref[...], preferred_element_type=F32)


def _ffn_body(x_ref, g_ref, wg_ref, wu_ref, wd_ref, o_ref):
    o_ref[...] = _swiglu_step(x_ref[...], g_ref, wg_ref, wu_ref, wd_ref)


def _ffn_specs(ffn_w, which):
    def pick(arr):
        return pl.BlockSpec((None, None) + arr.shape[2:], lambda *_: which + (0, 0), pipeline_mode=pl.Buffered(1))
    return [pick(arr) for arr in ffn_w]


def _ffn(x2, ffn_w, which, tm):
    t, d = x2.shape
    row = pl.BlockSpec((tm, d), lambda i: (i, 0))
    return pl.pallas_call(
        _ffn_body,
        out_shape=jax.ShapeDtypeStruct((t, d), F32),
        grid=(t // tm,),
        in_specs=[row] + _ffn_specs(ffn_w, which),
        out_specs=row,
        compiler_params=_params("parallel"),
        name="ffn",
    )(x2, *ffn_w)


def _rwkv_prep_tile(aw, za, mu_ref, w0_ref, w2h_ref, w2l_ref, a0_ref, a2_ref, g2_ref, kk_ref_w, ka_ref,
                    rk_ref, sel_ref, r_out, lw_out, k_out, v_out, kk_out, a_out, g_out, bonus_out, prev_ref):
    tm = za.shape[0]
    row = lax.broadcasted_iota(jnp.int32, za.shape, 0)
    shifted = jnp.where(row == 0, prev_ref[0:1, :], pltpu.roll(za, 1, 0))
    prev_ref[0:1, :] = za[tm - 1:tm, :]
    xr = za + (shifted - za) * mu_ref[...]

    r = xr[:, 0:aw]
    k = xr[:, aw:2 * aw]
    v = xr[:, 2 * aw:3 * aw]
    lr = xr[:, 3 * aw:3 * aw + DECAY_RANK + ICL_RANK]
    gd = xr[:, 3 * aw + DECAY_RANK + ICL_RANK:]

    pre = w0_ref[...] + _dot_x3(jnp.tanh(lr), w2h_ref[...], w2l_ref[...])
    softplus_neg = jnp.maximum(-pre, 0.0) + jnp.log(1.0 + jnp.exp(-jnp.abs(pre)))
    w_log = -softplus_neg - 0.5
    lw_out[...] = -jnp.exp(w_log)

    a = _sigmoid(a0_ref[...] + jnp.dot(lr.astype(BF16), a2_ref[...], preferred_element_type=F32))
    g_out[...] = jnp.dot(_sigmoid(gd).astype(BF16), g2_ref[...], preferred_element_type=F32)

    sel = sel_ref[...]
    kk = k * kk_ref_w[...]
    kk = kk / jnp.maximum(jnp.sqrt(_dot_sel(kk * kk, sel)), 1e-12)
    kmod = k * (1.0 + (a - 1.0) * ka_ref[...])
    bonus_out[...] = _dot_sel(r * kmod * rk_ref[...], sel) * v
    r_out[...] = r
    k_out[...] = kmod
    v_out[...] = v
    kk_out[...] = kk
    a_out[...] = a


def _conv_tile(bw, zb, bias_ref, dw_ref, dwb_ref, lnw_ref, lnb_ref, y_ref, buf_ref, shift_ref):
    tm = zb.shape[0]
    u = zb + bias_ref[...]
    gl = u[:, :bw] * _sigmoid(u[:, bw:])
    buf_ref[0:CONV_HALO, :] = buf_ref[tm:tm + CONV_HALO, :]
    buf_ref[CONV_HALO:, :] = gl
    base = CONV_HALO - (CONV_WIDTH - 1)
    acc = jnp.zeros((tm, bw), F32)
    for shift in range(8):
        taps = [tap for tap in range(CONV_WIDTH) if (base + tap) % 8 == shift]
        if not taps:
            continue
        span = (base + taps[-1]) // 8 * 8
        if shift:
            shift_ref[shift - 1, 0:tm + span, :] = buf_ref[pl.ds(shift, tm + span), :]
        for tap in taps:
            off = (base + tap) // 8 * 8
            rows = shift_ref[shift - 1, off:off + tm, :] if shift else buf_ref[off:off + tm, :]
            acc = acc + dw_ref[tap:tap + 1, :] * rows
    c = acc + dwb_ref[...]
    mu = jnp.mean(c, axis=-1, keepdims=True)
    var = jnp.mean(jnp.square(c - mu), axis=-1, keepdims=True)
    y = (c - mu) * lax.rsqrt(var + 1e-5) * lnw_ref[...] + lnb_ref[...]
    y_ref[...] = y * _sigmoid(y)


N_PREP_IN = 11
N_CONV_IN = 5


def _even_front_body(aw, a_proj, bw, x_ref, g_ref, w_ref, *refs):
    prep_in, refs = refs[:N_PREP_IN], refs[N_PREP_IN:]
    conv_in, refs = refs[:N_CONV_IN], refs[N_CONV_IN:]
    prep_out, yb_out, (prev_ref, buf_ref, shift_ref) = refs[:8], refs[8], refs[9:]
    tm = x_ref.shape[0]

    @pl.when(pl.program_id(1) == 0)
    def _():
        prev_ref[...] = jnp.zeros_like(prev_ref)
        buf_ref[tm:tm + CONV_HALO, :] = jnp.zeros((CONV_HALO, bw), F32)

    h = _rms(x_ref[...], g_ref[...], 1e-6).astype(BF16)
    zb = jnp.dot(h, w_ref[:, :2 * bw], preferred_element_type=F32)
    za = jnp.dot(h, w_ref[:, 2 * bw:], preferred_element_type=F32)
    _conv_tile(bw, zb, *conv_in, yb_out, buf_ref, shift_ref)
    _rwkv_prep_tile(aw, za, *prep_in, *prep_out, prev_ref)


def _even_front(x2, bsz, seq, g, w, a_proj, prep_params, conv_params, tm):
    t, d = x2.shape
    n = w.shape[1]
    aw = prep_params[1].shape[1]
    bw = (n - a_proj) // 2
    nt = seq // tm
    row = lambda b, j: (b * nt + j, 0)
    whole = lambda arr: _const_spec(arr.shape)
    assert len(prep_params) == N_PREP_IN and len(conv_params) == N_CONV_IN
    return pl.pallas_call(
        functools.partial(_even_front_body, aw, a_proj, bw),
        out_shape=(jax.ShapeDtypeStruct((t, aw), F32),) * 8 + (jax.ShapeDtypeStruct((t, bw), F32),),
        grid=(bsz, nt),
        in_specs=[pl.BlockSpec((tm, d), row), _const_spec((1, d)), _const_spec((d, n))]
                 + [whole(p) for p in prep_params] + [whole(p) for p in conv_params],
        out_specs=(pl.BlockSpec((tm, aw), row),) * 8 + (pl.BlockSpec((tm, bw), row),),
        scratch_shapes=[pltpu.VMEM((8, a_proj), F32), pltpu.VMEM((tm + CONV_HALO, bw), F32),
                        pltpu.VMEM((7, tm + CONV_HALO - 8, bw), F32)],
        compiler_params=_params("parallel", "arbitrary"),
        name="even_front",
    )(x2, g.reshape(1, d), w, *prep_params, *conv_params)


def _wkv_body(r_ref, lw_ref, k_ref, v_ref, kk_ref, a_ref, y_ref, state_ref):
    nseq, ch, width = r_ref.shape
    n_pairs = width // LANES
    c = pl.program_id(1)

    @pl.when(c == 0)
    def _():
        state_ref[...] = jnp.zeros_like(state_ref)

    ti = lax.broadcasted_iota(jnp.int32, (ch, ch), 0)
    tj = lax.broadcasted_iota(jnp.int32, (ch, ch), 1)
    tri = (ti >= tj).astype(BF16)
    lane = lax.broadcasted_iota(jnp.int32, (ch, LANES), 1)
    head0 = lane < HEAD_DIM

    def stack(x):
        zero = jnp.zeros_like(x)
        return jnp.concatenate([jnp.where(head0, x, zero), jnp.where(head0, zero, x)], axis=0)

    n = 2 * ch
    ri = lax.broadcasted_iota(jnp.int32, (n, n), 0) & (ch - 1)
    ci = lax.broadcasted_iota(jnp.int32, (n, n), 1) & (ch - 1)
    strict = ri > ci
    incl = ri >= ci
    eye = (lax.broadcasted_iota(jnp.int32, (n, n), 0) == lax.broadcasted_iota(jnp.int32, (n, n), 1)).astype(F32)
    own_head = ((lax.broadcasted_iota(jnp.int32, (n, LANES), 0) < ch)
                == (lax.broadcasted_iota(jnp.int32, (n, LANES), 1) < HEAD_DIM))

    rd, ad, bd, kd, bhd, khd, vd, g_last, unit_pos = [], [], [], [], [], [], [], [], []
    for s in range(nseq):
        lw = lw_ref[s]
        lg = sum(jnp.dot(tri, part, preferred_element_type=F32) for part in _split3(lw))
        lg_last = lg[ch - 1:ch, :]
        e_pos = jnp.exp(lg)
        e_neg = jnp.exp(-lg)
        e_prev = jnp.exp(lg - lw)
        e_tail = jnp.exp(lg_last - lg)
        kk = kk_ref[s]
        kka = kk * a_ref[s]
        kmod = k_ref[s]
        r_t = (r_ref[s] * e_pos).astype(BF16)
        a_t = (-kk * e_prev).astype(BF16)
        b_t = (kka * e_neg).astype(BF16)
        k_t = (kmod * e_neg).astype(BF16)
        b_h = (kka * e_tail).astype(BF16)
        k_h = (kmod * e_tail).astype(BF16)
        v_b = v_ref[s].astype(BF16)
        g_s = jnp.exp(lg_last)
        for p in range(n_pairs):
            sl = slice(p * LANES, (p + 1) * LANES)
            rd.append(stack(r_t[:, sl]))
            ad.append(stack(a_t[:, sl]))
            bd.append(stack(b_t[:, sl]))
            kd.append(stack(k_t[:, sl]))
            bhd.append(stack(b_h[:, sl]))
            khd.append(stack(k_h[:, sl]))
            vd.append(stack(v_b[:, sl]))
            g_last.append(g_s[:, sl])
            unit_pos.append((s, p, sl))
    units = range(len(unit_pos))

    ob = [_dot_nt(jnp.concatenate([ad[u], rd[u]], axis=0), bd[u]) for u in units]
    ok = [_dot_nt(jnp.concatenate([ad[u], rd[u]], axis=0), kd[u]) for u in units]
    a_ab = [jnp.where(strict, ob[u][:n], 0.0) for u in units]
    a_rb = [jnp.where(incl, ob[u][n:], 0.0).astype(BF16) for u in units]
    a_ak = [jnp.where(strict, ok[u][:n], 0.0) for u in units]
    a_rk = [jnp.where(incl, ok[u][n:], 0.0).astype(BF16) for u in units]
    akv = [_dot(a_ak[u], vd[u]) for u in units]

    tinv = [eye + a_ab[u] for u in units]
    pw = [_dot(a_ab[u], a_ab[u]) for u in units]
    for _ in range(int(math.log2(ch)) - 2):
        both = [_dot(jnp.concatenate([tinv[u], pw[u]], axis=0), pw[u]) for u in units]
        tinv = [tinv[u] + both[u][:n] for u in units]
        pw = [both[u][n:] for u in units]
    tinv = [tinv[u] + _dot(tinv[u], pw[u]) for u in units]

    w12 = [_dot(tinv[u], jnp.concatenate([ad[u], akv[u].astype(BF16)], axis=1)) for u in units]
    s_old = [state_ref[s, p] for s, p, _ in unit_pos]
    from_state = [_dot_nt(jnp.concatenate([w12[u][:, :LANES].astype(BF16), rd[u]], axis=0), s_old[u]) for u in units]
    uv = [jnp.concatenate([(from_state[u][:n] + w12[u][:, LANES:]).astype(BF16), vd[u]], axis=0) for u in units]
    y = [from_state[u][n:] + _dot(jnp.concatenate([a_rb[u], a_rk[u]], axis=1), uv[u]) for u in units]
    for u, (s, p, sl) in enumerate(unit_pos):
        mean = jnp.sum(y[u], axis=-1, keepdims=True) * (1.0 / HEAD_DIM)
        dev = jnp.where(own_head, y[u] - mean, 0.0)
        var = jnp.sum(dev * dev, axis=-1, keepdims=True) * (1.0 / HEAD_DIM)
        yn = dev * lax.rsqrt(var + GN_EPS)
        y_ref[s, :, sl] = yn[:ch] + yn[ch:]
        state_ref[s, p] = s_old[u] * g_last[u] + _dot_tn(uv[u], jnp.concatenate([bhd[u], khd[u]], axis=0))


WKV_SEQS = 8


def _wkv(r, lw, k, v, kk, a, bsz, seq):
    width = r.shape[-1]
    nc = seq // CHUNK
    nseq = WKV_SEQS if bsz % WKV_SEQS == 0 else 1
    spec = pl.BlockSpec((nseq, CHUNK, width), lambda b, c: (b, c, 0))
    args = [t.reshape(bsz, seq, width) for t in (r, lw, k, v, kk, a)]
    y = pl.pallas_call(
        _wkv_body,
        out_shape=jax.ShapeDtypeStruct((bsz, seq, width), F32),
        grid=(bsz // nseq, nc),
        in_specs=[spec] * 6,
        out_specs=spec,
        scratch_shapes=[pltpu.VMEM((nseq, width // LANES, LANES, LANES), F32)],
        compiler_params=_params("parallel", "arbitrary"),
        name="wkv7_chunk",
    )(*args)
    return y.reshape(bsz * seq, width)


def _even_out_ffn_body(aw, y_ref, bonus_ref, gate_ref, yb_ref, x_ref, lnw_ref, lnb_ref, w_ref,
                       g_ref, wg_ref, wu_ref, wd_ref, o_ref):
    yn = y_ref[...] * lnw_ref[...] + lnb_ref[...]
    ya = ((yn + bonus_ref[...]) * gate_ref[...]).astype(BF16)
    x = (x_ref[...]
         + jnp.dot(ya, w_ref[0:aw, :], preferred_element_type=F32)
         + jnp.dot(yb_ref[...].astype(BF16), w_ref[aw:, :], preferred_element_type=F32))
    o_ref[...] = _swiglu_step(x, g_ref, wg_ref, wu_ref, wd_ref)


def _even_out_ffn(y, bonus, gate, yb, x2, lnw, lnb, w, ffn_w, which, tm):
    t, d = x2.shape
    aw = y.shape[1]
    bw = yb.shape[1]
    row = lambda i: (i, 0)
    return pl.pallas_call(
        functools.partial(_even_out_ffn_body, aw),
        out_shape=jax.ShapeDtypeStruct((t, d), F32),
        grid=(t // tm,),
        in_specs=[pl.BlockSpec((tm, aw), row), pl.BlockSpec((tm, aw), row), pl.BlockSpec((tm, aw), row),
                  pl.BlockSpec((tm, bw), row), pl.BlockSpec((tm, d), row), _const_spec((1, aw)),
                  _const_spec((1, aw)), _const_spec((d, d))]
                 + _ffn_specs(ffn_w, which),
        out_specs=pl.BlockSpec((tm, d), row),
        compiler_params=_params("parallel"),
        name="even_out_ffn",
    )(y, bonus, gate, yb, x2, lnw, lnb, w, *ffn_w)


ODD_IN_SPLIT = 2


def _odd_in_body(d, x_ref, g_ref, w_ref, qc_ref, qs_ref, kc_ref, ks_ref, sel_ref, q_ref, k_ref, vt_ref):
    tm = x_ref.shape[0]
    sub = tm // ODD_IN_SPLIT
    sel = sel_ref[...]
    lane = lax.broadcasted_iota(jnp.int32, (sub, LANES), 1)
    first_half = (lane & (HEAD_DIM - 1)) < HEAD_DIM // 2
    zs = []
    for part in range(ODD_IN_SPLIT):
        rows = slice(part * sub, (part + 1) * sub)
        h = _rms(x_ref[rows, :], g_ref[...], 1e-6).astype(BF16)
        zs.append(jnp.dot(h, w_ref[...], preferred_element_type=F32))
    for part, z in enumerate(zs):
        rows = slice(part * sub, (part + 1) * sub)
        for blk in range(d // LANES):
            sl = slice(blk * LANES, (blk + 1) * LANES)
            for off, c_ref, s_ref, out_ref in ((0, qc_ref, qs_ref, q_ref), (d, kc_ref, ks_ref, k_ref)):
                xq = z[:, off + blk * LANES:off + (blk + 1) * LANES]
                ss = jnp.dot((xq * xq).astype(BF16), sel, preferred_element_type=F32)
                swapped = jnp.where(first_half, pltpu.roll(xq, LANES - HEAD_DIM // 2, 1),
                                    pltpu.roll(xq, HEAD_DIM // 2, 1))
                out = lax.rsqrt(ss + HEAD_DIM * 1e-6) * (xq * c_ref[rows, :] + swapped * s_ref[rows, :])
                out_ref[rows, sl] = out.astype(BF16)
        for hd in range(d // LANES):
            vt_ref[hd, :, rows] = z[:, 2 * d + hd * LANES:2 * d + (hd + 1) * LANES].T.astype(BF16)


def _odd_in(x2, bsz, seq, g, w, tabs, sel, tm):
    t, d = x2.shape
    nt = seq // tm
    heads = d // LANES
    row = lambda b, j: (b * nt + j, 0)
    pos = lambda b, j: (j, 0)
    out = jax.ShapeDtypeStruct((t, d), BF16)
    return pl.pallas_call(
        functools.partial(_odd_in_body, d),
        out_shape=(out, out, jax.ShapeDtypeStruct((bsz, heads, nt, LANES, tm), BF16)),
        grid=(bsz, nt),
        in_specs=[pl.BlockSpec((tm, d), row), _const_spec((1, d)), _const_spec((d, 3 * d))]
                 + [pl.BlockSpec((tm, LANES), pos)] * 4 + [_const_spec((LANES, LANES))],
        out_specs=(pl.BlockSpec((tm, d), row), pl.BlockSpec((tm, d), row),
                   pl.BlockSpec((None, heads, None, LANES, tm), lambda b, j: (b, 0, j, 0, 0))),
        compiler_params=_params("parallel", "parallel"),
        name="odd_in",
    )(x2, g.reshape(1, d), w, *tabs, sel)


ATTN_TQ = 512
ATTN_TK = 512
ATTN_GROUP = 3
SUM_ROWS = 16


def _attn_body(lam_init, q_ref, k_ref, vt_ref, lam_ref, sn_ref, o_ref, qd_buf, s_buf, p_buf, al_buf, m_ref, acc_ref):
    seq = q_ref.shape[0]
    tk = vt_ref.shape[-1]
    tq = qd_buf.shape[1] // 2
    per_q = tq // tk
    ones = jnp.ones((SUM_ROWS, tk), BF16)
    lv = lam_ref[...]
    lam = (jnp.exp(jnp.sum(lv[0:1] * lv[1:2], axis=-1, keepdims=True))
           - jnp.exp(jnp.sum(lv[2:3] * lv[3:4], axis=-1, keepdims=True)) + lam_init)
    lane = lax.broadcasted_iota(jnp.int32, (tq, LANES), 1)
    comp0 = lane < HEAD_DIM

    def scores(i, j, slot):
        if j == 0:
            q = q_ref[i * tq:(i + 1) * tq, :]
            zero = jnp.zeros_like(q)
            qd_buf[i & 1] = jnp.concatenate([jnp.where(comp0, q, zero), jnp.where(comp0, zero, q)], axis=0)
            m_ref[i & 1] = jnp.full(m_ref.shape[1:], -1e30, F32)
            acc_ref[i & 1] = jnp.zeros(acc_ref.shape[1:], F32)
        kj = k_ref[j * tk:(j + 1) * tk, :]
        s_buf[slot] = lax.dot_general(kj, qd_buf[i & 1], (((1,), (1,)), ((), ())),
                                      preferred_element_type=F32)

    def softmax(i, j, slot):
        s = s_buf[slot]
        first_diag = per_q * i
        if j >= first_diag:
            ki = lax.broadcasted_iota(jnp.int32, s.shape, 0) + (j - first_diag) * tk
            qi = lax.broadcasted_iota(jnp.int32, s.shape, 1) & (tq - 1)
            s = jnp.where(ki <= qi, s, -jnp.inf)
        m_old = m_ref[i & 1]
        m_new = jnp.maximum(m_old, jnp.max(s, axis=0, keepdims=True))
        m_ref[i & 1] = m_new
        al_buf[slot] = jnp.exp2(m_old - m_new)
        p_buf[slot] = jnp.exp2(s - m_new).astype(BF16)

    def values(i, j, slot):
        vj = jnp.concatenate([vt_ref[j], ones], axis=0)
        acc_ref[i & 1] = al_buf[slot] * acc_ref[i & 1] + jnp.dot(vj, p_buf[slot], preferred_element_type=F32)
        if j == per_q * (i + 1) - 1:
            acc = acc_ref[i & 1]
            o = acc[:LANES] / acc[LANES:LANES + 1]
            out = (o[:, :tq] - lam * o[:, tq:]).T
            out = _rms(out, sn_ref[...], 1e-5) * (1.0 - lam_init)
            o_ref[i * tq:(i + 1) * tq, :] = out.astype(BF16)

    blocks = [(i, j) for i in range(seq // tq) for j in range(per_q * (i + 1))]
    ring = s_buf.shape[0]
    group = ring // 2
    n_groups = -(-len(blocks) // group)
    for u in range(n_groups + 2):
        for stage, lag in ((values, 2), (softmax, 1), (scores, 0)):
            for g in range((u - lag) * group, (u - lag + 1) * group):
                if 0 <= g < len(blocks):
                    stage(*blocks[g], g % ring)


def _attn(q3, k3, vt5, lam_vecs, sub_norm, lam_init, tq):
    bsz, seq, d = q3.shape
    heads = d // LANES
    nkv, _, tk = vt5.shape[2:]
    assert tq % tk == 0 and seq % tq == 0
    ring = 2 * ATTN_GROUP
    head_rows = pl.BlockSpec((None, seq, LANES), lambda b, h: (b, 0, h))
    return pl.pallas_call(
        functools.partial(_attn_body, lam_init),
        out_shape=jax.ShapeDtypeStruct((bsz, seq, d), BF16),
        grid=(bsz, heads),
        in_specs=[head_rows, head_rows,
                  pl.BlockSpec((None, None, nkv, LANES, tk), lambda b, h: (b, h, 0, 0, 0)),
                  _const_spec((4, HEAD_DIM)), _const_spec((1, LANES))],
        out_specs=head_rows,
        scratch_shapes=[pltpu.VMEM((2, 2 * tq, LANES), BF16), pltpu.VMEM((ring, tk, 2 * tq), F32),
                        pltpu.VMEM((ring, tk, 2 * tq), BF16), pltpu.VMEM((ring, 1, 2 * tq), F32),
                        pltpu.VMEM((2, 1, 2 * tq), F32), pltpu.VMEM((2, LANES + SUM_ROWS, 2 * tq), F32)],
        compiler_params=_params("parallel", "parallel"),
        name="diff_attn",
    )(q3, k3, vt5, lam_vecs, sub_norm)


def _proj_ffn_body(o_ref, x_ref, w_ref, g_ref, wg_ref, wu_ref, wd_ref, out_ref):
    x = x_ref[...] + jnp.dot(o_ref[...], w_ref[...], preferred_element_type=F32)
    out_ref[...] = _swiglu_step(x, g_ref, wg_ref, wu_ref, wd_ref)


def _proj_ffn(o2, x2, w, ffn_w, which, tm):
    t, d = x2.shape
    row = pl.BlockSpec((tm, d), lambda i: (i, 0))
    return pl.pallas_call(
        _proj_ffn_body,
        out_shape=jax.ShapeDtypeStruct((t, d), F32),
        grid=(t // tm,),
        in_specs=[row, row, _const_spec((d, d))] + _ffn_specs(ffn_w, which),
        out_specs=row,
        compiler_params=_params("parallel"),
        name="proj_ffn",
    )(o2, x2, w, *ffn_w)


def _block_ones(n, blk):
    idx = jnp.arange(n) // blk
    return (idx[:, None] == idx[None, :]).astype(BF16)


def _rope_tables(seq):
    inv = 1.0 / (ROPE_THETA ** (jnp.arange(0, HEAD_DIM, 2, dtype=F32) / HEAD_DIM))
    ang = jnp.arange(seq, dtype=F32)[:, None] * inv[None, :]
    cos, sin = jnp.cos(ang), jnp.sin(ang)
    reps = LANES // HEAD_DIM
    return jnp.tile(jnp.concatenate([cos, cos], -1), (1, reps)), jnp.tile(jnp.concatenate([-sin, sin], -1), (1, reps))


def _qk_tables(cos, sin, norm, scale):
    c = scale * math.sqrt(HEAD_DIM)
    reps = LANES // HEAD_DIM
    gain = jnp.tile(norm, reps) * c
    gain_swapped = jnp.tile(jnp.roll(norm, HEAD_DIM // 2), reps) * c
    return cos * gain[None, :], sin * gain_swapped[None, :]


EVEN_FRONT_TM = 512


def kernel(x, ffn_norm, ffn_w_gate, ffn_w_up, ffn_w_down, mix_norm, a_w_in, a_mu, a_w0, a_w2, a_a0, a_a2, a_g2, a_k_k, a_k_a, a_r_k, a_ln_w, a_ln_b, b_glu_bias, b_dw, b_dw_bias, b_ln_w, b_ln_b, e_w_out, c_w_in, c_q_norm, c_k_norm, c_lq1, c_lk1, c_lq2, c_lk2, c_sub_norm, c_w_out):
    bsz, seq, d = x.shape
    depth = ffn_norm.shape[0]
    t = bsz * seq
    tm = min(EVEN_FRONT_TM, seq)
    tm_proj = min(FFN_TM, seq)
    x2 = x.reshape(t, d)
    cos, sin = _rope_tables(seq)

    ffn_w = (ffn_norm.reshape(depth, 2, 1, d), ffn_w_gate.astype(BF16), ffn_w_up.astype(BF16),
             ffn_w_down.astype(BF16))

    for layer in range(depth):
        x2 = _ffn(x2, ffn_w, (layer, 0), tm_proj)
        j = layer // 2
        if layer % 2 == 0:
            aw = a_w0.shape[1]
            a_proj = a_mu.shape[1]
            row = lambda v: v.reshape(1, -1)
            sel = _block_ones(aw, HEAD_DIM)
            w2p = jnp.pad(a_w2[j], ((0, ICL_RANK), (0, 0)))
            w2h = w2p.astype(BF16)
            w2l = (w2p - w2h.astype(F32)).astype(BF16)
            a2p = jnp.pad(a_a2[j], ((DECAY_RANK, 0), (0, 0))).astype(BF16)
            prep_params = (row(a_mu[j]), row(a_w0[j]), w2h, w2l, row(a_a0[j]), a2p, a_g2[j].astype(BF16),
                           row(a_k_k[j]), row(a_k_a[j]), row(a_r_k[j]), sel)
            conv_params = (row(b_glu_bias[j]), jnp.pad(b_dw[j], ((0, CONV_HALO - CONV_WIDTH), (0, 0))),
                           row(b_dw_bias[j]), row(b_ln_w[j]), row(b_ln_b[j]))
            w_in = jnp.concatenate([a_w_in[j][:, a_proj:], a_w_in[j][:, :a_proj]], axis=1).astype(BF16)
            r, lw, k, v, kk, a, g, bonus, yb = _even_front(
                x2, bsz, seq, mix_norm[layer], w_in, a_proj, prep_params, conv_params, tm)
            y = _wkv(r, lw, k, v, kk, a, bsz, seq)
            x2 = _even_out_ffn(y, bonus, g, yb, x2, row(a_ln_w[j]), row(a_ln_b[j]), e_w_out[j].astype(BF16),
                               ffn_w, (layer, 1), tm_proj)
        else:
            lam_init = 0.8 - 0.6 * math.exp(-0.3 * layer)
            tabs = (_qk_tables(cos, sin, c_q_norm[j], ATTN_SCALE * math.log2(math.e))
                    + _qk_tables(cos, sin, c_k_norm[j], 1.0))
            q, k, vt5 = _odd_in(x2, bsz, seq, mix_norm[layer], c_w_in[j].astype(BF16), tabs,
                                _block_ones(LANES, HEAD_DIM), min(ATTN_TK, seq))
            lam_vecs = jnp.stack([c_lq1[j], c_lk1[j], c_lq2[j], c_lk2[j]])
            o = _attn(q.reshape(bsz, seq, d), k.reshape(bsz, seq, d), vt5, lam_vecs,
                      c_sub_norm[j].reshape(1, LANES), lam_init, ATTN_TQ)
            x2 = _proj_ffn(o.reshape(t, d), x2, c_w_out[j].astype(BF16), ffn_w, (layer, 1), tm_proj)
    return x2.reshape(bsz, seq, d)
```
